```python
import math
import jax, jax.numpy as jnp
from jax import lax
import numpy as np

D_MODEL = 4096
BATCH = 32
SEQ = 256
DEPTH = 4
DEC_BATCH = 2
DEC_SEQ = 1024
PAST_LEN = 512

GRID_W = 64
N_EVEN = (DEPTH + 1) // 2
N_ODD = DEPTH // 2
N_MOD = 6
EPS = 1e-6
D_A = D_MODEL // 2
HEAD_A = 64
H_A = D_A // HEAD_A
W_LORA = 96
A_LORA = 96
G_LORA = 256
LO_COLS = 2 * W_LORA + 2 * A_LORA + G_LORA
N_A_COLS = 3 * D_A + LO_COLS
DECAY_SCALE = 0.606531
GN_EPS = 64e-5
D_B = D_MODEL - D_A
IN_AB = N_A_COLS + 3 * D_B
D_C = D_MODEL
HYENA_ORDER = 2
FILT_BANDS = 16
FILT_EMB = 2 * FILT_BANDS + 1
FILT_HIDDEN = 64
N_FILT_INNER = 2
DECAY_TARGET = 1e-2
FAST_DECAY_PCT = 0.3
SLOW_DECAY_PCT = 1.5
MIN_DECAY = math.log(DECAY_TARGET) / FAST_DECAY_PCT
MAX_DECAY = math.log(DECAY_TARGET) / SLOW_DECAY_PCT
N_EXPERTS = 32
TOP_K = 4
D_EXPERT = 1536
SWIGLU_ALPHA = 1.702
SWIGLU_LIMIT = 7.0
GROUP_BLOCK = 128

kernel_name = 'hybrid_rwkv7_shortconv_hyena_moe_flow_step'


def rmsnorm(x, g):
    xf = x.astype(jnp.float32)
    y = xf * lax.rsqrt(jnp.mean(xf * xf, axis=-1, keepdims=True) + EPS)
    return (y * g.astype(jnp.float32)).astype(x.dtype)


def neighbours(z, row_w):
    L = z.shape[1]
    zp = jnp.pad(z, [(0, 0), (1, 1)] + [(0, 0)] * (z.ndim - 2))
    prev, nxt = zp[:, :-2], zp[:, 2:]
    if row_w is not None:
        col = jnp.arange(L) % row_w
        shape = (1, L) + (1,) * (z.ndim - 2)
        prev = jnp.where((col != 0).reshape(shape), prev, 0)
        nxt = jnp.where((col != row_w - 1).reshape(shape), nxt, 0)
    return prev, nxt


def conv3(z, w3, row_w):
    prev, nxt = neighbours(z, row_w)
    return prev * w3[0] + z * w3[1] + nxt * w3[2]


def to_col_major(z):
    B, L = z.shape[:2]
    rows = L // GRID_W
    return z.reshape((B, rows, GRID_W) + z.shape[2:]).swapaxes(1, 2).reshape(z.shape)


def from_col_major(z):
    B, L = z.shape[:2]
    rows = L // GRID_W
    return z.reshape((B, GRID_W, rows) + z.shape[2:]).swapaxes(1, 2).reshape(z.shape)


def rwkv7_scan(s0, r, w, k, v, a, b, reverse):
    def step(s, inp):
        r_t, w_t, k_t, v_t, a_t, b_t = inp
        sa = jnp.einsum('bhvk,bhk->bhv', s, a_t)
        s = s * w_t[:, :, None, :] + sa[..., None] * b_t[:, :, None, :] + v_t[..., None] * k_t[:, :, None, :]
        return s, jnp.einsum('bhvk,bhk->bhv', s, r_t)
    xs = tuple(jnp.swapaxes(t, 0, 1) for t in (r, w, k, v, a, b))
    s, o = lax.scan(step, s0, xs, reverse=reverse)
    return s, jnp.swapaxes(o, 0, 1)


def mixer_ab(h, s0, row_w, col_major, p):
    f32 = jnp.float32
    Bn, L, _ = h.shape
    z = h @ p['ab_w_in']
    za, zb = z[..., :N_A_COLS], z[..., N_A_COLS:]
    prev, nxt = neighbours(za, row_w)
    za = za + p['ab_shift_mu'][0] * (prev - za) + p['ab_shift_mu'][1] * (nxt - za)
    cuts = np.cumsum([D_A, D_A, D_A, W_LORA, W_LORA, A_LORA, A_LORA]).tolist()
    r, k, v, w_lo_f, w_lo_b, a_lo_f, a_lo_b, g_lo = jnp.split(za, cuts, axis=-1)
    heads = lambda t: t.reshape(Bn, L, H_A, HEAD_A).astype(f32)
    rh, vh = heads(r), heads(v)
    kk = heads(k * p['a_k_k'])
    kk = kk / jnp.maximum(jnp.sqrt(jnp.sum(kk * kk, axis=-1, keepdims=True)), 1e-12)
    g = jax.nn.sigmoid(g_lo) @ p['a_g_up']
    outs, bonuses, finals = [], [], []
    for d, (w_lo, a_lo) in enumerate(((w_lo_f, a_lo_f), (w_lo_b, a_lo_b))):
        decay = jnp.exp(-DECAY_SCALE * jax.nn.sigmoid(p['a_w0'][d] + jnp.tanh(w_lo) @ p['a_w_up'][d]))
        a = jax.nn.sigmoid(p['a_a0'][d] + a_lo @ p['a_a_up'][d])
        k_d = heads(k * (1.0 + (a - 1.0) * p['a_k_a']))
        seq = (rh, heads(decay), k_d, vh, -kk, kk * heads(a))
        if col_major:
            seq = tuple(to_col_major(t) for t in seq)
        s_fin, o = rwkv7_scan(s0[:, d].astype(f32), *seq, reverse=(d == 1))
        if col_major:
            o = from_col_major(o)
        outs.append(o)
        bonuses.append(jnp.sum(rh * k_d * p['a_r_k'].astype(f32), axis=-1, keepdims=True) * vh)
        finals.append(s_fin)
    o = outs[0] + outs[1]
    mean = jnp.mean(o, axis=-1, keepdims=True)
    var = jnp.mean(jnp.square(o - mean), axis=-1, keepdims=True)
    on = (o - mean) * lax.rsqrt(var + GN_EPS)
    on = on * p['a_ln_w'].reshape(H_A, HEAD_A).astype(f32) + p['a_ln_b'].reshape(H_A, HEAD_A).astype(f32)
    ya = (on + bonuses[0] + bonuses[1]).reshape(Bn, L, D_A).astype(h.dtype) * g
    bg, cg, hin = jnp.split(zb, [D_B, 2 * D_B], axis=-1)
    yb = bg * conv3(cg * hin, p['b_conv'], row_w)
    out = jnp.concatenate([ya, yb], axis=-1) @ p['ab_w_out']
    return out, jnp.stack(finals, axis=1)


def hyena_filters(L, p):
    f32 = jnp.float32
    t = jnp.linspace(0.0, 1.0, L, dtype=f32)[:, None]
    ang = (2.0 * math.pi / L) * jnp.arange(L, dtype=f32)[:, None]
    bands = jnp.linspace(1e-4, FILT_BANDS - 1, FILT_BANDS, dtype=f32)
    feat = jnp.concatenate([t, jnp.cos(bands * ang), -jnp.sin(bands * ang)], axis=-1)
    freq = p['c_f_freq'].astype(f32)
    hid = jnp.sin(freq[0] * (feat @ p['c_f_w0'].astype(f32) + p['c_f_b0'].astype(f32)))
    for j in range(N_FILT_INNER):
        hid = jnp.sin(freq[j + 1] * (hid @ p['c_f_w1'][j].astype(f32) + p['c_f_b1'][j].astype(f32)))
    filt = (hid @ p['c_f_out'].astype(f32)).reshape(L, HYENA_ORDER, 2, D_C)
    deltas = jnp.abs(jnp.linspace(MIN_DECAY, MAX_DECAY, D_C, dtype=f32))
    window = jnp.exp(-t * deltas)
    return filt * window[:, None, None, :]


def bidir_fft_conv(u, hf, hb, bias):
    L = u.shape[1]
    n = 2 * L
    kern = jnp.concatenate([hf, jnp.zeros_like(hf[:1]), hb[:0:-1]], axis=0)
    uf = u.astype(jnp.float32)
    y = jnp.fft.irfft(jnp.fft.rfft(uf, n=n, axis=1) * jnp.fft.rfft(kern, n=n, axis=0)[None], n=n, axis=1)[:, :L]
    return (y + uf * bias.astype(jnp.float32)).astype(u.dtype)


def mixer_hyena(h, row_w, p):
    L = h.shape[1]
    z = conv3(h @ p['c_w_in'], p['c_short_w'], row_w) + p['c_short_b']
    v, x1, x2 = jnp.split(z, 3, axis=-1)
    filt = hyena_filters(L, p)
    u = v
    for o, gate in enumerate((x1, x2)):
        u = gate * bidir_fft_conv(u, filt[:, o, 0], filt[:, o, 1], p['c_bias'][o])
    return u @ p['c_w_out']


def moe_ffn(h, router_w, router_b, w1, b1, w2, b2):
    f32 = jnp.float32
    shp = h.shape
    x = h.reshape(-1, shp[-1])
    T = x.shape[0]
    logits = x.astype(f32) @ router_w.astype(f32) + router_b.astype(f32)
    top_v, top_i = lax.top_k(logits, TOP_K)
    gates = jax.nn.softmax(top_v, axis=-1)
    n_assign = T * TOP_K
    n_blocks = -(-(n_assign + N_EXPERTS * (GROUP_BLOCK - 1)) // GROUP_BLOCK)
    n_slots = n_blocks * GROUP_BLOCK
    flat_e = top_i.reshape(-1)
    order = jnp.argsort(flat_e)
    e_sorted = flat_e[order]
    counts = jnp.zeros((N_EXPERTS,), jnp.int32).at[flat_e].add(1)
    padded = (counts + GROUP_BLOCK - 1) // GROUP_BLOCK * GROUP_BLOCK
    end_pad = jnp.cumsum(padded)
    start_pad = end_pad - padded
    start = jnp.cumsum(counts) - counts
    slot = start_pad[e_sorted] + jnp.arange(n_assign, dtype=jnp.int32) - start[e_sorted]
    tok = jnp.zeros((n_slots,), jnp.int32).at[slot].set((order // TOP_K).astype(jnp.int32))
    gate_slot = jnp.zeros((n_slots,), f32).at[slot].set(gates.reshape(-1)[order])
    blk_e = jnp.minimum(jnp.searchsorted(end_pad, jnp.arange(n_blocks, dtype=jnp.int32) * GROUP_BLOCK, side='right'), N_EXPERTS - 1)
    xs = x[tok].reshape(n_blocks, GROUP_BLOCK, shp[-1])

    def expert_block(args):
        xb, e = args
        u = xb @ w1[e] + b1[e]
        glu = jnp.minimum(u[:, :D_EXPERT], SWIGLU_LIMIT)
        lin = jnp.clip(u[:, D_EXPERT:], -SWIGLU_LIMIT, SWIGLU_LIMIT)
        act = glu * jax.nn.sigmoid(SWIGLU_ALPHA * glu) * (lin + 1.0)
        return act @ w2[e] + b2[e]

    ys = lax.map(expert_block, (xs, blk_e)).reshape(n_slots, shp[-1])
    out = jnp.zeros((T, shp[-1]), f32).at[tok].add(ys.astype(f32) * gate_slot[:, None])
    return out.reshape(shp).astype(h.dtype)


def trunk(x, cond, s0, latent, shared, ab, hy, moe):
    row_w = GRID_W if latent else None
    finals = []
    for l in range(DEPTH):
        mod = jax.nn.silu(cond) @ shared['ada_w'][l] + shared['ada_b'][l]
        mod = mod.reshape(cond.shape[0], N_MOD, 1, D_MODEL)
        h = rmsnorm(x, shared['norm1_w'][l]) * (1.0 + mod[:, 1]) + mod[:, 0]
        if l % 2 == 0:
            i = l // 2
            out, s_fin = mixer_ab(h, s0[:, i], row_w, latent and i % 2 == 1, {k: v[i] for k, v in ab.items()})
            finals.append(s_fin)
        else:
            out = mixer_hyena(h, row_w, {k: v[l // 2] for k, v in hy.items()})
        x = x + mod[:, 2] * out
        h = rmsnorm(x, shared['norm2_w'][l]) * (1.0 + mod[:, 4]) + mod[:, 3]
        x = x + mod[:, 5] * moe_ffn(h, *(w[l] for w in moe))
    return rmsnorm(x, shared['final_norm_w']), jnp.stack(finals, axis=1)


def setup_inputs(seed: int = 0) -> dict:
    key = jax.random.key(seed)
    ks = iter(jax.random.split(key, 64))
    f32 = jnp.float32
    nrm = lambda shape, s: jax.random.normal(next(ks), shape, f32) * s
    unif = lambda shape, lo, hi: jax.random.uniform(next(ks), shape, f32, lo, hi)
    gain = lambda shape: 1.0 + nrm(shape, 0.02)
    return {
        'x_prompt': nrm((BATCH, SEQ, D_MODEL), 1.0),
        'x_sample': nrm((DEC_BATCH, DEC_SEQ, D_MODEL), 1.0),
        'state_rwkv': nrm((DEC_BATCH, N_EVEN, 2, H_A, HEAD_A, HEAD_A), 0.3),
        'c': nrm((DEC_BATCH, D_MODEL), 1.0),
        'c_ctx': nrm((D_MODEL,), 1.0),
        'ada_w': nrm((DEPTH, D_MODEL, N_MOD * D_MODEL), 0.5 * D_MODEL ** -0.5),
        'ada_b': nrm((DEPTH, N_MOD * D_MODEL), 0.02),
        'norm1_w': gain((DEPTH, D_MODEL)),
        'norm2_w': gain((DEPTH, D_MODEL)),
        'final_norm_w': gain((D_MODEL,)),
        'ab_w_in': nrm((N_EVEN, D_MODEL, IN_AB), D_MODEL ** -0.5),
        'ab_shift_mu': unif((N_EVEN, 2, N_A_COLS), 0.0, 0.5),
        'a_w0': unif((N_EVEN, 2, D_A), -3.0, 1.0),
        'a_w_up': nrm((N_EVEN, 2, W_LORA, D_A), 0.5 * W_LORA ** -0.5),
        'a_a0': nrm((N_EVEN, 2, D_A), 0.5),
        'a_a_up': nrm((N_EVEN, 2, A_LORA, D_A), 0.5 * A_LORA ** -0.5),
        'a_g_up': nrm((N_EVEN, G_LORA, D_A), G_LORA ** -0.5),
        'a_k_k': 0.85 + nrm((N_EVEN, D_A), 0.02),
        'a_k_a': gain((N_EVEN, D_A)),
        'a_r_k': nrm((N_EVEN, H_A, HEAD_A), 0.1),
        'a_ln_w': gain((N_EVEN, D_A)),
        'a_ln_b': nrm((N_EVEN, D_A), 0.02),
        'b_conv': nrm((N_EVEN, 3, D_B), 3 ** -0.5),
        'ab_w_out': nrm((N_EVEN, D_A + D_B, D_MODEL), D_MODEL ** -0.5),
        'c_w_in': nrm((N_ODD, D_MODEL, 3 * D_C), D_MODEL ** -0.5),
        'c_short_w': nrm((N_ODD, 3, 3 * D_C), 3 ** -0.5),
        'c_short_b': nrm((N_ODD, 3 * D_C), 0.02),
        'c_f_w0': nrm((N_ODD, FILT_EMB, FILT_HIDDEN), FILT_EMB ** -0.5),
        'c_f_b0': nrm((N_ODD, FILT_HIDDEN), 0.1),
        'c_f_w1': nrm((N_ODD, N_FILT_INNER, FILT_HIDDEN, FILT_HIDDEN), FILT_HIDDEN ** -0.5),
        'c_f_b1': nrm((N_ODD, N_FILT_INNER, FILT_HIDDEN), 0.1),
        'c_f_freq': gain((N_ODD, N_FILT_INNER + 1, FILT_HIDDEN)),
        'c_f_out': nrm((N_ODD, FILT_HIDDEN, HYENA_ORDER * 2 * D_C), 0.1 * FILT_HIDDEN ** -0.5),
        'c_bias': nrm((N_ODD, HYENA_ORDER, D_C), 0.1),
        'c_w_out': nrm((N_ODD, D_C, D_MODEL), D_C ** -0.5),
        'moe_router_w': nrm((DEPTH, D_MODEL, N_EXPERTS), D_MODEL ** -0.5),
        'moe_router_b': nrm((DEPTH, N_EXPERTS), 0.01),
        'moe_w1': nrm((DEPTH, N_EXPERTS, D_MODEL, 2 * D_EXPERT), D_MODEL ** -0.5),
        'moe_b1': nrm((DEPTH, N_EXPERTS, 2 * D_EXPERT), 0.01),
        'moe_w2': nrm((DEPTH, N_EXPERTS, D_EXPERT, D_MODEL), D_EXPERT ** -0.5),
        'moe_b2': nrm((DEPTH, N_EXPERTS, D_MODEL), 0.01),
    }


def reference(x_prompt, x_sample, state_rwkv, c, c_ctx, ada_w, ada_b, norm1_w, norm2_w, final_norm_w,
              ab_w_in, ab_shift_mu, a_w0, a_w_up, a_a0, a_a_up, a_g_up, a_k_k, a_k_a, a_r_k, a_ln_w, a_ln_b,
              b_conv, ab_w_out, c_w_in, c_short_w, c_short_b, c_f_w0, c_f_b0, c_f_w1, c_f_b1, c_f_freq,
              c_f_out, c_bias, c_w_out, moe_router_w, moe_router_b, moe_w1, moe_b1, moe_w2, moe_b2):
    shared = dict(ada_w=ada_w, ada_b=ada_b, norm1_w=norm1_w, norm2_w=norm2_w, final_norm_w=final_norm_w)
    ab = dict(ab_w_in=ab_w_in, ab_shift_mu=ab_shift_mu, a_w0=a_w0, a_w_up=a_w_up, a_a0=a_a0, a_a_up=a_a_up,
              a_g_up=a_g_up, a_k_k=a_k_k, a_k_a=a_k_a, a_r_k=a_r_k, a_ln_w=a_ln_w, a_ln_b=a_ln_b,
              b_conv=b_conv, ab_w_out=ab_w_out)
    hy = dict(c_w_in=c_w_in, c_short_w=c_short_w, c_short_b=c_short_b, c_f_w0=c_f_w0, c_f_b0=c_f_b0,
              c_f_w1=c_f_w1, c_f_b1=c_f_b1, c_f_freq=c_f_freq, c_f_out=c_f_out, c_bias=c_bias, c_w_out=c_w_out)
    moe = (moe_router_w, moe_router_b, moe_w1, moe_b1, moe_w2, moe_b2)
    s0_ctx = jnp.zeros((x_prompt.shape[0], N_EVEN, 2, H_A, HEAD_A, HEAD_A), jnp.float32)
    y_prompt, state_new = trunk(x_prompt, c_ctx[None, :], s0_ctx, False, shared, ab, hy, moe)
    y_sample, _ = trunk(x_sample, c, state_rwkv, True, shared, ab, hy, moe)
    return (y_prompt, y_sample, state_new.astype(x_prompt.dtype))
```

```python
import functools
import math
import types

import numpy as np
import jax
import jax.numpy as jnp
from jax import lax
from jax.experimental import pallas as pl
from jax.experimental.pallas import tpu as pltpu

F32 = jnp.float32
BF16 = jnp.bfloat16
HI = lax.Precision.HIGHEST

HEAD = 64
LANES = 128
ROW_TILE = 256
N_MOD = 6
EPS = 1e-6
GN_EPS = 64e-5
DECAY_SCALE = 0.606531
SWIGLU_ALPHA = 1.702
SWIGLU_LIMIT = 7.0
DECAY_TARGET = 1e-2
MIN_DECAY = math.log(DECAY_TARGET) / 0.3
MAX_DECAY = math.log(DECAY_TARGET) / 1.5
N_FILT_INNER = 2
HYENA_ORDER = 2
MOD_ROWS = 16
VMEM_LIMIT = 56 * 1024 * 1024
NEG = -1e30


def _make_cfg(d_model=4096, batch=32, seq=256, depth=4, dec_batch=2, dec_seq=1024, grid_w=64,
              n_experts=32, top_k=4, d_expert=1536, w_lora=96, a_lora=96, g_lora=256,
              filt_bands=16, filt_hidden=64, moe_rows=512, moe_sub=256, moe_bf=256):
    c = types.SimpleNamespace()
    c.d = d_model
    c.batch, c.seq, c.depth, c.dec_batch, c.dec_seq, c.grid_w = batch, seq, depth, dec_batch, dec_seq, grid_w
    c.n_even, c.n_odd = (depth + 1) // 2, depth // 2
    c.d_a = d_model // 2
    c.h_a = c.d_a // HEAD
    c.w_lora, c.a_lora, c.g_lora = w_lora, a_lora, g_lora
    c.n_a_cols = 3 * c.d_a + 2 * w_lora + 2 * a_lora + g_lora
    c.d_b = d_model - c.d_a
    c.d_c = d_model
    c.filt_emb = 2 * filt_bands + 1
    c.filt_bands, c.filt_hidden = filt_bands, filt_hidden
    c.n_experts, c.top_k, c.d_expert = n_experts, top_k, d_expert
    c.t_ctx, c.t_lat = batch * seq, dec_batch * dec_seq
    c.t = c.t_ctx + c.t_lat
    c.lo_pad = 4 * LANES + g_lora
    c.moe_rows, c.moe_sub, c.moe_bf = moe_rows, moe_sub, moe_bf
    c.n_assign = c.t * top_k
    c.n_chunks = n_experts + c.n_assign // moe_rows
    assert seq % ROW_TILE == 0 and dec_seq % ROW_TILE == 0 and ROW_TILE % grid_w == 0
    assert seq & (seq - 1) == 0 and grid_w & (grid_w - 1) == 0 and c.t_ctx % dec_seq == 0
    assert 1 + dec_batch <= MOD_ROWS and n_experts <= LANES
    return c


def _params(sem):
    return pltpu.CompilerParams(dimension_semantics=sem, vmem_limit_bytes=VMEM_LIMIT)


def _cond_row(row0, cfg):
    return jnp.where(row0 < cfg.t_ctx, 0, 1 + jnp.maximum(row0 - cfg.t_ctx, 0) // cfg.dec_seq)


def _neighbours(y, row0, cfg):
    n = y.shape[0]
    period = jnp.where(row0 < cfg.t_ctx, cfg.seq, cfg.grid_w)
    pos = lax.broadcasted_iota(jnp.int32, y.shape, 0) & (period - 1)
    prev = jnp.where(pos != 0, pltpu.roll(y, 1, 0), 0.0)
    nxt = jnp.where(pos != period - 1, pltpu.roll(y, n - 1, 0), 0.0)
    return prev, nxt


def _ada_kernel(c_ref, w_ref, b_ref, o_ref):
    c = c_ref[...]
    s = c * jax.nn.sigmoid(c)
    o_ref[...] = jnp.dot(s.astype(BF16), w_ref[...].astype(BF16), preferred_element_type=F32) + b_ref[...]


def _ada(cond, ada_w, ada_b, cfg, bn=512):
    depth, d, n = ada_w.shape
    bn = min(bn, n)
    return pl.pallas_call(
        _ada_kernel,
        grid=(depth, n // bn),
        in_specs=[pl.BlockSpec((MOD_ROWS, d), lambda l, j: (0, 0)),
                  pl.BlockSpec((None, d, bn), lambda l, j: (l, 0, j)),
                  pl.BlockSpec((None, 1, bn), lambda l, j: (l, 0, j))],
        out_specs=pl.BlockSpec((None, MOD_ROWS, bn), lambda l, j: (l, 0, j)),
        out_shape=jax.ShapeDtypeStruct((depth, MOD_ROWS, n), F32),
        compiler_params=_params(("arbitrary", "arbitrary")),
        name="ada_mod",
    )(cond, ada_w, ada_b.reshape(depth, 1, n))


def _norm_kernel(x_ref, g_ref, *rest, cfg, modulate):
    o_ref = rest[-1]
    x = x_ref[...]
    y = x * lax.rsqrt(jnp.mean(x * x, axis=-1, keepdims=True) + EPS) * g_ref[...]
    if modulate:
        sh_ref, sc_ref = rest[0], rest[1]
        r = _cond_row(pl.program_id(0) * ROW_TILE, cfg)
        y = y * (1.0 + sc_ref[pl.ds(r, 1), :]) + sh_ref[pl.ds(r, 1), :]
    o_ref[...] = y.astype(o_ref.dtype)


def _norm(x, g, cfg, mod=None, shift_blk=0, scale_blk=1, out_dtype=BF16):
    t, d = x.shape
    in_specs = [pl.BlockSpec((ROW_TILE, d), lambda i: (i, 0)), pl.BlockSpec((1, d), lambda i: (0, 0))]
    args = [x, g.reshape(1, d)]
    if mod is not None:
        in_specs += [pl.BlockSpec((MOD_ROWS, d), lambda i: (0, shift_blk)),
                     pl.BlockSpec((MOD_ROWS, d), lambda i: (0, scale_blk))]
        args += [mod, mod]
    return pl.pallas_call(
        functools.partial(_norm_kernel, cfg=cfg, modulate=mod is not None),
        grid=(t // ROW_TILE,),
        in_specs=in_specs,
        out_specs=pl.BlockSpec((ROW_TILE, d), lambda i: (i, 0)),
        out_shape=jax.ShapeDtypeStruct((t, d), out_dtype),
        compiler_params=_params(("arbitrary",)),
        name="rmsnorm_mod",
    )(*args)


def _norm_router_kernel(x_ref, g_ref, sh_ref, sc_ref, rw_ref, rb_ref, h_ref, ti_ref, tg_ref, *, cfg):
    x = x_ref[...]
    y = x * lax.rsqrt(jnp.mean(x * x, axis=-1, keepdims=True) + EPS) * g_ref[...]
    r = _cond_row(pl.program_id(0) * ROW_TILE, cfg)
    h = y * (1.0 + sc_ref[pl.ds(r, 1), :]) + sh_ref[pl.ds(r, 1), :]
    h_ref[...] = h
    logits = jnp.dot(h, rw_ref[...], preferred_element_type=F32, precision=HI) + rb_ref[...]
    lane = lax.broadcasted_iota(jnp.int32, logits.shape, 1).astype(F32)
    vals, idxs = [], []
    for _ in range(cfg.top_k):
        m = jnp.max(logits, axis=-1, keepdims=True)
        ij = jnp.min(jnp.where(logits == m, lane, float(LANES)), axis=-1, keepdims=True)
        vals.append(m)
        idxs.append(ij)
        logits = jnp.where(lane == ij, NEG, logits)
    es = [jnp.exp(v - vals[0]) for v in vals]
    tot = es[0]
    for e in es[1:]:
        tot = tot + e
    ti = jnp.zeros_like(lane)
    tg = jnp.zeros_like(lane)
    for j in range(cfg.top_k):
        ti = jnp.where(lane == float(j), idxs[j], ti)
        tg = jnp.where(lane == float(j), es[j] / tot, tg)
    ti_ref[...] = ti.astype(jnp.int32)
    tg_ref[...] = tg


def _norm_router(x, g, mod, router_w, router_b, cfg):
    t, d = x.shape
    e = cfg.n_experts
    rw = jnp.pad(router_w, ((0, 0), (0, LANES - e)))
    rb = jnp.pad(router_b, (0, LANES - e), constant_values=NEG).reshape(1, LANES)
    row = lambda i: (i, 0)
    return pl.pallas_call(
        functools.partial(_norm_router_kernel, cfg=cfg),
        grid=(t // ROW_TILE,),
        in_specs=[pl.BlockSpec((ROW_TILE, d), row), pl.BlockSpec((1, d), lambda i: (0, 0)),
                  pl.BlockSpec((MOD_ROWS, d), lambda i: (0, 3)), pl.BlockSpec((MOD_ROWS, d), lambda i: (0, 4)),
                  pl.BlockSpec((d, LANES), lambda i: (0, 0)), pl.BlockSpec((1, LANES), lambda i: (0, 0))],
        out_specs=[pl.BlockSpec((ROW_TILE, d), row), pl.BlockSpec((ROW_TILE, LANES), row),
                   pl.BlockSpec((ROW_TILE, LANES), row)],
        out_shape=[jax.ShapeDtypeStruct((t, d), F32), jax.ShapeDtypeStruct((t, LANES), jnp.int32),
                   jax.ShapeDtypeStruct((t, LANES), F32)],
        compiler_params=_params(("arbitrary",)),
        name="rmsnorm_router",
    )(x, g.reshape(1, d), mod, mod, rw, rb)


def _mm_kernel(x_ref, w_ref, *rest, cfg, bm, has_taps, has_res, hi):
    o_ref = rest[-1]
    if hi:
        y = jnp.dot(x_ref[...], w_ref[...], preferred_element_type=F32, precision=HI)
    else:
        y = jnp.dot(x_ref[...].astype(BF16), w_ref[...].astype(BF16), preferred_element_type=F32)
    row0 = pl.program_id(1) * bm
    k = 0
    if has_taps:
        taps = rest[k][...]
        k += 1
        prev, nxt = _neighbours(y, row0, cfg)
        y = prev * taps[0:1] + y * taps[1:2] + nxt * taps[2:3] + taps[3:4]
    if has_res:
        res_ref, gate_ref = rest[k], rest[k + 1]
        y = res_ref[...] + gate_ref[pl.ds(_cond_row(row0, cfg), 1), :] * y
    o_ref[...] = y.astype(o_ref.dtype)


def _mm(x, w, cfg, *, bm=512, bn=1024, out_dtype=F32, taps=None, res=None, gate=None, gate_blk=0, hi=False):
    m, kd = x.shape
    n = w.shape[1]
    bm, bn = min(bm, m), min(bn, n)
    assert m % bm == 0 and n % bn == 0
    in_specs = [pl.BlockSpec((bm, kd), lambda j, i: (i, 0)), pl.BlockSpec((kd, bn), lambda j, i: (0, j))]
    args = [x, w]
    if taps is not None:
        assert bm % cfg.seq == 0 and cfg.t_ctx % bm == 0
        in_specs.append(pl.BlockSpec((8, bn), lambda j, i: (0, j)))
        args.append(taps)
    if res is not None:
        assert cfg.t_ctx % bm == 0 and cfg.dec_seq % bm == 0
        nb = n // bn
        in_specs += [pl.BlockSpec((bm, bn), lambda j, i: (i, j)),
                     pl.BlockSpec((MOD_ROWS, bn), lambda j, i: (0, gate_blk * nb + j))]
        args += [res, gate]
    return pl.pallas_call(
        functools.partial(_mm_kernel, cfg=cfg, bm=bm, has_taps=taps is not None, has_res=res is not None, hi=hi),
        grid=(n // bn, m // bm),
        in_specs=in_specs,
        out_specs=pl.BlockSpec((bm, bn), lambda j, i: (i, j)),
        out_shape=jax.ShapeDtypeStruct((m, n), out_dtype),
        compiler_params=_params(("arbitrary", "arbitrary")),
        name="matmul",
    )(*args)


def _taps(t0, t1, t2, bias):
    z = jnp.zeros_like(t0)
    return jnp.stack([t0, t1, t2, bias, z, z, z, z]).astype(F32)


def _seg_mats(cfg):
    head = jnp.arange(cfg.d_a, dtype=jnp.int32) // HEAD
    seg = (head[:, None] == jnp.arange(LANES, dtype=jnp.int32)[None, :]).astype(F32)
    return seg, seg.T


def _head_sum(x, seg_ref, segt_ref):
    s = jnp.dot(x, seg_ref[...], preferred_element_type=F32, precision=HI)
    return jnp.dot(s, segt_ref[...], preferred_element_type=F32, precision=HI)


def _rwkv_prep_kernel(r_ref, k_ref, v_ref, lo_ref, kk_s, ka_s, rk_s, w0_ref, a0_ref, wup_ref, aup_ref, gup_ref,
                      seg_ref, segt_ref, a_out, w_out, k_out, b_out, bonus_out, g_out):
    r, k, v, lo = r_ref[...], k_ref[...], v_ref[...], lo_ref[...]
    kk = k * kk_s[...]
    norm = jnp.sqrt(_head_sum(kk * kk, seg_ref, segt_ref))
    kk = kk / jnp.maximum(norm, 1e-12)
    a_out[...] = -kk
    g_lo = lo[:, 4 * LANES:]
    g_out[...] = jnp.dot(jax.nn.sigmoid(g_lo).astype(BF16), gup_ref[...], preferred_element_type=F32)
    bonus = jnp.zeros_like(r)
    for d in range(2):
        w_lo = lo[:, d * LANES:(d + 1) * LANES]
        a_lo = lo[:, (2 + d) * LANES:(3 + d) * LANES]
        wz = w0_ref[d:d + 1, :] + jnp.dot(jnp.tanh(w_lo).astype(BF16), wup_ref[d], preferred_element_type=F32)
        w_out[d] = jnp.exp(-DECAY_SCALE * jax.nn.sigmoid(wz))
        a = jax.nn.sigmoid(a0_ref[d:d + 1, :] + jnp.dot(a_lo.astype(BF16), aup_ref[d], preferred_element_type=F32))
        k_d = k * (1.0 + (a - 1.0) * ka_s[...])
        k_out[d] = k_d
        b_out[d] = kk * a
        bonus = bonus + _head_sum(r * k_d * rk_s[...], seg_ref, segt_ref) * v
    bonus_out[...] = bonus


def _rwkv_prep(rkv, lo, p, cfg):
    t = rkv.shape[0]
    da = cfg.d_a
    seg, segt = _seg_mats(cfg)
    row = lambda i: (i, 0)
    full2 = lambda i: (0, 0)
    full3 = lambda i: (0, 0, 0)
    pad_k = lambda w: jnp.pad(w, ((0, 0), (0, LANES - w.shape[1]), (0, 0))).astype(BF16)
    out2 = jax.ShapeDtypeStruct((t, da), F32)
    out3 = jax.ShapeDtypeStruct((2, t, da), F32)
    tr = ROW_TILE // 2
    spec3 = pl.BlockSpec((2, tr, da), lambda i: (0, i, 0))
    return pl.pallas_call(
        _rwkv_prep_kernel,
        grid=(t // tr,),
        in_specs=[pl.BlockSpec((tr, da), lambda i: (i, 0)), pl.BlockSpec((tr, da), lambda i: (i, 1)),
                  pl.BlockSpec((tr, da), lambda i: (i, 2)), pl.BlockSpec((tr, cfg.lo_pad), row),
                  pl.BlockSpec((1, da), full2), pl.BlockSpec((1, da), full2), pl.BlockSpec((1, da), full2),
                  pl.BlockSpec((2, da), full2), pl.BlockSpec((2, da), full2),
                  pl.BlockSpec((2, LANES, da), full3), pl.BlockSpec((2, LANES, da), full3),
                  pl.BlockSpec((cfg.g_lora, da), full2),
                  pl.BlockSpec((da, LANES), full2), pl.BlockSpec((LANES, da), full2)],
        out_specs=[pl.BlockSpec((tr, da), row), spec3, spec3, spec3,
                   pl.BlockSpec((tr, da), row), pl.BlockSpec((tr, da), row)],
        out_shape=[out2, out3, out3, out3, out2, out2],
        compiler_params=_params(("arbitrary",)),
        name="rwkv_prep",
    )(rkv, rkv, rkv, lo, p['a_k_k'].reshape(1, da), p['a_k_a'].reshape(1, da), p['a_r_k'].reshape(1, da),
      p['a_w0'], p['a_a0'], pad_k(p['a_w_up']), pad_k(p['a_a_up']), p['a_g_up'].astype(BF16), seg, segt)


def _scan_kernel(r_ref, w_ref, k_ref, v_ref, a_ref, b_ref, s0_ref, o_ref, sf_ref, s_scr, *, tb):
    step_blk = pl.program_id(1)

    @pl.when(step_blk == 0)
    def _():
        s_scr[...] = s0_ref[...]

    def step(t, carry):
        sa0 = jnp.zeros((HEAD, LANES), F32)
        sa1 = jnp.zeros((HEAD, LANES), F32)
        for k in range(0, HEAD, 2):
            sa0 = sa0 + s_scr[k] * a_ref[t, pl.ds(k, 1), :]
            sa1 = sa1 + s_scr[k + 1] * a_ref[t, pl.ds(k + 1, 1), :]
        sa = sa0 + sa1
        vv = v_ref[t]
        o0 = jnp.zeros((HEAD, LANES), F32)
        o1 = jnp.zeros((HEAD, LANES), F32)
        for k in range(HEAD):
            sk = (s_scr[k] * w_ref[t, pl.ds(k, 1), :] + sa * b_ref[t, pl.ds(k, 1), :]
                  + vv * k_ref[t, pl.ds(k, 1), :])
            s_scr[k] = sk
            if k % 2 == 0:
                o0 = o0 + sk * r_ref[t, pl.ds(k, 1), :]
            else:
                o1 = o1 + sk * r_ref[t, pl.ds(k, 1), :]
        o_ref[t] = o0 + o1
        return carry

    lax.fori_loop(0, tb, step, 0)

    @pl.when(step_blk == pl.num_programs(1) - 1)
    def _():
        sf_ref[...] = s_scr[...]


def _scan(seqs, s0, tb=16):
    g, l = seqs[0].shape[:2]
    tb = min(tb, l)
    seq_spec = pl.BlockSpec((None, tb, HEAD, LANES), lambda gi, ti: (gi, ti, 0, 0))
    st_spec = pl.BlockSpec((None, HEAD, HEAD, LANES), lambda gi, ti: (gi, 0, 0, 0))
    return pl.pallas_call(
        functools.partial(_scan_kernel, tb=tb),
        grid=(g, l // tb),
        in_specs=[seq_spec] * 6 + [st_spec],
        out_specs=[seq_spec, st_spec],
        out_shape=[jax.ShapeDtypeStruct((g, l, HEAD, LANES), F32), jax.ShapeDtypeStruct((g, HEAD, HEAD, LANES), F32)],
        scratch_shapes=[pltpu.VMEM((HEAD, HEAD, LANES), F32)],
        compiler_params=_params(("arbitrary", "arbitrary")),
        name="rwkv7_scan",
    )(*seqs, s0)


def _ab_post_kernel(o_ref, bonus_ref, g_ref, bg_ref, cg_ref, hin_ref, lnw_ref, lnb_ref, bconv_ref, seg_ref, segt_ref,
                    ya_ref, yb_ref, *, cfg):
    o = o_ref[...]
    mean = _head_sum(o, seg_ref, segt_ref) * (1.0 / HEAD)
    c = o - mean
    var = _head_sum(c * c, seg_ref, segt_ref) * (1.0 / HEAD)
    on = c * lax.rsqrt(var + GN_EPS) * lnw_ref[...] + lnb_ref[...]
    ya_ref[...] = ((on + bonus_ref[...]) * g_ref[...]).astype(ya_ref.dtype)
    z = cg_ref[...] * hin_ref[...]
    prev, nxt = _neighbours(z, pl.program_id(0) * ROW_TILE, cfg)
    w3 = bconv_ref[...]
    yb_ref[...] = (bg_ref[...] * (prev * w3[0:1] + z * w3[1:2] + nxt * w3[2:3])).astype(yb_ref.dtype)


def _ab_post(o, bonus, g, zb, p, cfg):
    t = o.shape[0]
    da = cfg.d_a
    seg, segt = _seg_mats(cfg)
    row = lambda i: (i, 0)
    full2 = lambda i: (0, 0)
    half = pl.BlockSpec((ROW_TILE, da), row)
    bconv = jnp.pad(p['b_conv'], ((0, 5), (0, 0)))
    ya, yb = pl.pallas_call(
        functools.partial(_ab_post_kernel, cfg=cfg),
        grid=(t // ROW_TILE,),
        in_specs=[half, half, half,
                  pl.BlockSpec((ROW_TILE, da), lambda i: (i, 0)), pl.BlockSpec((ROW_TILE, da), lambda i: (i, 1)),
                  pl.BlockSpec((ROW_TILE, da), lambda i: (i, 2)),
                  pl.BlockSpec((1, da), full2), pl.BlockSpec((1, da), full2), pl.BlockSpec((8, da), full2),
                  pl.BlockSpec((da, LANES), full2), pl.BlockSpec((LANES, da), full2)],
        out_specs=[half, half],
        out_shape=[jax.ShapeDtypeStruct((t, da), BF16)] * 2,
        compiler_params=_params(("arbitrary",)),
        name="ab_post",
    )(o, bonus, g, zb, zb, zb, p['a_ln_w'].reshape(1, da), p['a_ln_b'].reshape(1, da), bconv, seg, segt)
    return jnp.concatenate([ya, yb], axis=-1)


def _col_major(z, cfg):
    b, l = z.shape[:2]
    return z.reshape((b, l // cfg.grid_w, cfg.grid_w) + z.shape[2:]).swapaxes(1, 2).reshape(z.shape)


def _row_major(z, cfg):
    b, l = z.shape[:2]
    return z.reshape((b, cfg.grid_w, l // cfg.grid_w) + z.shape[2:]).swapaxes(1, 2).reshape(z.shape)


def _to_chains(xf, xb, nb, l, col_major, cfg):
    xf = xf.reshape(nb, l, cfg.h_a, HEAD)
    xb = xb.reshape(nb, l, cfg.h_a, HEAD)
    if col_major:
        xf, xb = _col_major(xf, cfg), _col_major(xb, cfg)
    st = jnp.stack([xf, xb[:, ::-1]])
    n_ch = 2 * nb * cfg.h_a
    st = st.transpose(2, 4, 0, 1, 3).reshape(l, HEAD, n_ch)
    pad = -n_ch % LANES
    st = jnp.pad(st, ((0, 0), (0, 0), (0, pad)))
    return st.reshape(l, HEAD, (n_ch + pad) // LANES, LANES).transpose(2, 0, 1, 3)


def _from_chains(o, nb, l, col_major, cfg):
    n_ch = 2 * nb * cfg.h_a
    o = o.transpose(1, 2, 0, 3).reshape(l, HEAD, -1)[:, :, :n_ch]
    o = o.reshape(l, HEAD, 2, nb, cfg.h_a).transpose(2, 3, 0, 4, 1)
    of, ob = o[0], o[1][:, ::-1]
    if col_major:
        of, ob = _row_major(of, cfg), _row_major(ob, cfg)
    return (of + ob).reshape(nb * l, cfg.d_a)


def _state_to_chains(s, cfg):
    nb = s.shape[0]
    n_ch = 2 * nb * cfg.h_a
    s = s.transpose(4, 3, 1, 0, 2).reshape(HEAD, HEAD, n_ch)
    pad = -n_ch % LANES
    s = jnp.pad(s, ((0, 0), (0, 0), (0, pad)))
    return s.reshape(HEAD, HEAD, (n_ch + pad) // LANES, LANES).transpose(2, 0, 1, 3)


def _state_from_chains(s, nb, cfg):
    n_ch = 2 * nb * cfg.h_a
    s = s.transpose(1, 2, 0, 3).reshape(HEAD, HEAD, -1)[:, :, :n_ch]
    return s.reshape(HEAD, HEAD, 2, nb, cfg.h_a).transpose(3, 2, 4, 1, 0)


def _rwkv_scans(rkv, a_vec, w2, k2, b2, s0_lat, col_major, cfg):
    da = cfg.d_a
    r, v = rkv[:, :da], rkv[:, 2 * da:3 * da]
    outs, final = [], None
    for lo_row, hi_row, nb, l, cm, s0 in ((0, cfg.t_ctx, cfg.batch, cfg.seq, False, None),
                                           (cfg.t_ctx, cfg.t, cfg.dec_batch, cfg.dec_seq, col_major, s0_lat)):
        sl = slice(lo_row, hi_row)
        shared = lambda x: _to_chains(x[sl], x[sl], nb, l, cm, cfg)
        paired = lambda x: _to_chains(x[0, sl], x[1, sl], nb, l, cm, cfg)
        seqs = (shared(r), paired(w2), paired(k2), shared(v), shared(a_vec), paired(b2))
        if s0 is None:
            s0c = jnp.zeros((seqs[0].shape[0], HEAD, HEAD, LANES), F32)
        else:
            s0c = _state_to_chains(s0.astype(F32), cfg)
        o, sf = _scan(seqs, s0c)
        outs.append(_from_chains(o, nb, l, cm, cfg))
        if s0 is None:
            final = _state_from_chains(sf, nb, cfg)
    return jnp.concatenate(outs, axis=0), final


def _mixer_ab(h, x, mod, s0_lat, col_major, p, cfg):
    da, nac = cfg.d_a, cfg.n_a_cols
    w_in = p['ab_w_in'].astype(BF16)
    mu = p['ab_shift_mu']
    one = jnp.ones_like(mu[0])
    zero = jnp.zeros_like(mu[0])
    taps = _taps(mu[0], one - mu[0] - mu[1], mu[1], zero)
    rkv = _mm(h, w_in[:, :3 * da], cfg, taps=taps[:, :3 * da])
    cuts = np.cumsum([3 * da, cfg.w_lora, cfg.w_lora, cfg.a_lora, cfg.a_lora]).tolist()
    padw = lambda a, lo_c, hi_c, fill: jnp.pad(a[:, lo_c:hi_c], ((0, 0), (0, LANES - (hi_c - lo_c))),
                                                constant_values=fill)
    w_lo = jnp.concatenate([padw(w_in, cuts[i], cuts[i + 1], 0) for i in range(4)] + [w_in[:, cuts[4]:nac]], axis=1)
    t_lo = jnp.concatenate([padw(taps, cuts[i], cuts[i + 1], 0) for i in range(4)] + [taps[:, cuts[4]:nac]], axis=1)
    lo = _mm(h, w_lo, cfg, bn=cfg.lo_pad, taps=t_lo)
    zb = _mm(h, w_in[:, nac:], cfg)
    a_vec, w2, k2, b2, bonus, g = _rwkv_prep(rkv, lo, p, cfg)
    o, final = _rwkv_scans(rkv, a_vec, w2, k2, b2, s0_lat, col_major, cfg)
    y = _ab_post(o, bonus, g, zb, p, cfg)
    x = _mm(y, p['ab_w_out'].astype(BF16), cfg, res=x, gate=mod, gate_blk=2)
    return x, final


def _filt_kernel(feat_ref, w0_ref, b0_ref, w1_ref, b1_ref, freq_ref, wout_ref, win_ref, o_ref, *, cols_per_dir):
    hid = jnp.sin(freq_ref[0:1, :] * (jnp.dot(feat_ref[...], w0_ref[...], preferred_element_type=F32, precision=HI)
                                      + b0_ref[...]))
    for j in range(N_FILT_INNER):
        hid = jnp.sin(freq_ref[j + 1:j + 2, :] * (jnp.dot(hid, w1_ref[j], preferred_element_type=F32, precision=HI)
                                                  + b1_ref[j:j + 1, :]))
    f = jnp.dot(hid, wout_ref[...], preferred_element_type=F32, precision=HI) * win_ref[...]
    backward = lax.rem(pl.program_id(0) // cols_per_dir, 2) == 1
    row = lax.broadcasted_iota(jnp.int32, f.shape, 0)
    o_ref[...] = jnp.where(jnp.logical_and(backward, row == 0), 0.0, f)


def _hyena_filters(l, p, cfg, bn=1024):
    dc = cfg.d_c
    bn = min(bn, dc)
    t = jnp.linspace(0.0, 1.0, l, dtype=F32)[:, None]
    ang = (2.0 * math.pi / l) * jnp.arange(l, dtype=F32)[:, None]
    bands = jnp.linspace(1e-4, cfg.filt_bands - 1, cfg.filt_bands, dtype=F32)
    feat = jnp.concatenate([t, jnp.cos(bands * ang), -jnp.sin(bands * ang)], axis=-1)
    deltas = jnp.abs(jnp.linspace(MIN_DECAY, MAX_DECAY, dc, dtype=F32))
    window = jnp.exp(-t * deltas)
    hp = LANES
    pad2 = lambda a, r, c: jnp.pad(a.astype(F32), ((0, r - a.shape[0]), (0, c - a.shape[1])))
    n_out = HYENA_ORDER * 2 * dc
    full2 = lambda j: (0, 0)
    return pl.pallas_call(
        functools.partial(_filt_kernel, cols_per_dir=dc // bn),
        grid=(n_out // bn,),
        in_specs=[pl.BlockSpec((l, hp), full2), pl.BlockSpec((hp, hp), full2), pl.BlockSpec((1, hp), full2),
                  pl.BlockSpec((N_FILT_INNER, hp, hp), lambda j: (0, 0, 0)), pl.BlockSpec((N_FILT_INNER, hp), full2),
                  pl.BlockSpec((N_FILT_INNER + 1, hp), full2), pl.BlockSpec((hp, bn), lambda j: (0, j)),
                  pl.BlockSpec((l, bn), lambda j: (0, lax.rem(j, dc // bn)))],
        out_specs=pl.BlockSpec((l, bn), lambda j: (0, j)),
        out_shape=jax.ShapeDtypeStruct((l, n_out), F32),
        compiler_params=_params(("arbitrary",)),
        name="hyena_filter",
    )(pad2(feat, l, hp), pad2(p['c_f_w0'], hp, hp), pad2(p['c_f_b0'][None], 1, hp),
      jnp.pad(p['c_f_w1'].astype(F32), ((0, 0), (0, hp - cfg.filt_hidden), (0, hp - cfg.filt_hidden))),
      pad2(p['c_f_b1'], N_FILT_INNER, hp), pad2(p['c_f_freq'], N_FILT_INNER + 1, hp),
      pad2(p['c_f_out'], hp, n_out), window)


def _dft_mats(l):
    f = jnp.arange(l, dtype=jnp.int32)[:, None]
    s = jnp.arange(l, dtype=jnp.int32)[None, :]
    ang = lax.rem(f * s, 2 * l).astype(F32) * (math.pi / l)
    nyq = jnp.where(lax.rem(s, 2) == 0, 1.0, -1.0).astype(F32)
    cos, sin = jnp.cos(ang), jnp.sin(ang)
    fwd = jnp.concatenate([cos, jnp.where(f == 0, nyq, -sin)], axis=0)
    cf = jnp.where(f == 0, 1.0, 2.0)
    inv = jnp.concatenate([cos * cf, jnp.where(f == 0, nyq, -2.0 * sin)], axis=0).T * (1.0 / (2 * l))
    return fwd, inv


def _hconv_kernel(fwd_ref, inv_ref, u_ref, gate_ref, kr_ref, ki_ref, bias_ref, o_ref, *, l):
    u = u_ref[...]
    spec = jnp.dot(fwd_ref[...], u.astype(BF16), preferred_element_type=F32)
    ur, ui = spec[:l], spec[l:]
    kr, ki = kr_ref[...], ki_ref[...]
    dc_row = lax.broadcasted_iota(jnp.int32, ur.shape, 0) == 0
    yr = ur * kr - jnp.where(dc_row, 0.0, ui * ki)
    yi = jnp.where(dc_row, ui * ki, ur * ki + ui * kr)
    y = jnp.dot(inv_ref[...], jnp.concatenate([yr, yi], axis=0).astype(BF16), preferred_element_type=F32)
    o_ref[...] = gate_ref[...] * (y + u * bias_ref[...])


def _hconv(u_arr, u_blk, gate_arr, gate_blk, kr, ki, bias, prev, l, row_blk0, n_seq, fwd, inv, cfg, bc=512):
    dc = cfg.d_c
    bc = min(bc, dc)
    nb = dc // bc
    const = lambda j, b: (0, 0)
    in_specs = [pl.BlockSpec((2 * l, l), const), pl.BlockSpec((l, 2 * l), const),
                pl.BlockSpec((l, bc), lambda j, b: (row_blk0 + b, u_blk * nb + j)),
                pl.BlockSpec((l, bc), lambda j, b: (row_blk0 + b, gate_blk * nb + j)),
                pl.BlockSpec((l, bc), lambda j, b: (0, j)), pl.BlockSpec((l, bc), lambda j, b: (0, j)),
                pl.BlockSpec((1, bc), lambda j, b: (0, j))]
    args = [fwd, inv, u_arr, gate_arr, kr, ki, bias.reshape(1, dc)]
    aliases = {}
    if prev is not None:
        in_specs.append(pl.BlockSpec(memory_space=pl.ANY))
        args.append(prev)
        aliases = {len(args) - 1: 0}

    def body(*refs):
        _hconv_kernel(*refs[:7], refs[-1], l=l)

    return pl.pallas_call(
        body,
        grid=(nb, n_seq),
        in_specs=in_specs,
        out_specs=pl.BlockSpec((l, bc), lambda j, b: (row_blk0 + b, j)),
        out_shape=jax.ShapeDtypeStruct((cfg.t, dc), F32),
        input_output_aliases=aliases,
        compiler_params=_params(("arbitrary", "arbitrary")),
        name="hyena_conv",
    )(*args)


def _mixer_hyena(h, x, mod, p, cfg):
    dc = cfg.d_c
    w3 = p['c_short_w']
    z = _mm(h, p['c_w_in'].astype(BF16), cfg, taps=_taps(w3[0], w3[1], w3[2], p['c_short_b']))
    groups = []
    for l, row_blk0, n_seq in ((cfg.seq, 0, cfg.batch), (cfg.dec_seq, cfg.t_ctx // cfg.dec_seq, cfg.dec_batch)):
        fwd, inv = _dft_mats(l)
        filt = _hyena_filters(l, p, cfg)
        spec = _mm(fwd, filt, cfg, bm=min(512, 2 * l), bn=512, hi=True).reshape(2 * l, HYENA_ORDER, 2, dc)
        sf, sb = spec[:, :, 0], spec[:, :, 1]
        kr = sf[:l] + sb[:l]
        first = (jnp.arange(l) == 0)[:, None, None]
        ki = jnp.where(first, sf[l:] + sb[l:], sf[l:] - sb[l:])
        groups.append((l, row_blk0, n_seq, fwd.astype(BF16), inv.astype(BF16), kr, ki))
    u_arr, u_blk = z, 0
    for o in range(HYENA_ORDER):
        out = jnp.zeros((cfg.t, dc), F32)
        for l, row_blk0, n_seq, fwd, inv, kr, ki in groups:
            out = _hconv(u_arr, u_blk, z, 1 + o, kr[:, o], ki[:, o], p['c_bias'][o], out, l, row_blk0, n_seq,
                         fwd, inv, cfg)
        u_arr, u_blk = out, 0
    return _mm(u_arr, p['c_w_out'].astype(BF16), cfg, res=x, gate=mod, gate_blk=2)


def _moe_kernel(ce_ref, cn_ref, tok_hbm, dst_hbm, x_hbm, w1g_ref, w1l_ref, b1g_ref, b1l_ref, w2_ref, b2_ref,
                buf_hbm, xs, yacc, tok_s, dst_s, sem_idx, sem_g, sem_s, *, rows, sub, n_j):
    c, j = pl.program_id(0), pl.program_id(1)
    n = cn_ref[c]

    @pl.when(jnp.logical_and(c == 0, j == 0))
    def _():
        xs[...] = jnp.zeros_like(xs)

    def row_in(r):
        return pltpu.make_async_copy(x_hbm.at[pl.ds(tok_s[r], 1)], xs.at[pl.ds(r, 1)], sem_g)

    def row_out(r):
        return pltpu.make_async_copy(yacc.at[pl.ds(r, 1)], buf_hbm.at[pl.ds(dst_s[r], 1)], sem_s)

    def each_row(fn):
        def body(r, carry):
            fn(r)
            return carry
        lax.fori_loop(0, n, body, 0)

    @pl.when(jnp.logical_and(j == 0, n > 0))
    def _():
        tok_cp = pltpu.make_async_copy(tok_hbm.at[c], tok_s, sem_idx.at[0])
        dst_cp = pltpu.make_async_copy(dst_hbm.at[c], dst_s, sem_idx.at[1])
        tok_cp.start()
        dst_cp.start()
        tok_cp.wait()
        dst_cp.wait()
        each_row(lambda r: row_in(r).start())
        yacc[...] = jnp.broadcast_to(b2_ref[...], yacc.shape)
        each_row(lambda r: row_in(r).wait())

    for s in range(rows // sub):
        @pl.when(n > s * sub)
        def _(s=s):
            xb = xs[s * sub:(s + 1) * sub, :].astype(BF16)
            ug = jnp.dot(xb, w1g_ref[...].astype(BF16), preferred_element_type=F32) + b1g_ref[...]
            ul = jnp.dot(xb, w1l_ref[...].astype(BF16), preferred_element_type=F32) + b1l_ref[...]
            glu = jnp.minimum(ug, SWIGLU_LIMIT)
            lin = jnp.clip(ul, -SWIGLU_LIMIT, SWIGLU_LIMIT)
            act = glu * jax.nn.sigmoid(SWIGLU_ALPHA * glu) * (lin + 1.0)
            yacc[s * sub:(s + 1) * sub, :] += jnp.dot(act.astype(BF16), w2_ref[...].astype(BF16),
                                                      preferred_element_type=F32)

    @pl.when(jnp.logical_and(j == n_j - 1, n > 0))
    def _():
        each_row(lambda r: row_out(r).start())
        each_row(lambda r: row_out(r).wait())


def _moe_plan(top_i, cfg):
    rows, nc, t, k = cfg.moe_rows, cfg.n_chunks, cfg.t, cfg.top_k
    flat_e = top_i.reshape(-1)
    order = jnp.argsort(flat_e).astype(jnp.int32)
    counts = jnp.zeros((cfg.n_experts,), jnp.int32).at[flat_e].add(1)
    start = jnp.cumsum(counts) - counts
    n_ch = (counts + rows - 1) // rows
    ch_end = jnp.cumsum(n_ch)
    cid = jnp.arange(nc, dtype=jnp.int32)
    valid = cid < ch_end[-1]
    e_c = jnp.minimum(jnp.searchsorted(ch_end, cid, side='right'), cfg.n_experts - 1).astype(jnp.int32)
    e_c = jnp.where(valid, e_c, e_c[jnp.maximum(ch_end[-1] - 1, 0)])
    q = cid - (ch_end - n_ch)[e_c]
    n_c = jnp.where(valid, jnp.clip(counts[e_c] - q * rows, 0, rows), 0).astype(jnp.int32)
    pos = (start[e_c] + q * rows)[:, None] + jnp.arange(rows, dtype=jnp.int32)[None, :]
    src = order[jnp.clip(pos, 0, cfg.n_assign - 1)]
    live = jnp.arange(rows, dtype=jnp.int32)[None, :] < n_c[:, None]
    tok = jnp.where(live, src // k, 0).astype(jnp.int32)
    dst = jnp.where(live, (src % k) * t + src // k, 0).astype(jnp.int32)
    return e_c, n_c, tok, dst


def _moe_experts(h, top_i, layer, w1, b1, w2, b2, cfg):
    d, de, bf, rows = cfg.d, cfg.d_expert, cfg.moe_bf, cfg.moe_rows
    n_j = de // bf
    e_c, n_c, tok, dst = _moe_plan(top_i, cfg)
    jj = lambda c, j, ce, cn: jnp.where(cn[c] > 0, j, n_j - 1)
    grid_spec = pltpu.PrefetchScalarGridSpec(
        num_scalar_prefetch=2,
        grid=(cfg.n_chunks, n_j),
        in_specs=[pl.BlockSpec(memory_space=pl.ANY), pl.BlockSpec(memory_space=pl.ANY),
                  pl.BlockSpec(memory_space=pl.ANY),
                  pl.BlockSpec((None, None, d, bf), lambda c, j, ce, cn: (layer, ce[c], 0, jj(c, j, ce, cn))),
                  pl.BlockSpec((None, None, d, bf), lambda c, j, ce, cn: (layer, ce[c], 0, n_j + jj(c, j, ce, cn))),
                  pl.BlockSpec((None, None, 1, bf), lambda c, j, ce, cn: (layer, ce[c], 0, jj(c, j, ce, cn))),
                  pl.BlockSpec((None, None, 1, bf), lambda c, j, ce, cn: (layer, ce[c], 0, n_j + jj(c, j, ce, cn))),
                  pl.BlockSpec((None, None, bf, d), lambda c, j, ce, cn: (layer, ce[c], jj(c, j, ce, cn), 0)),
                  pl.BlockSpec((None, None, 1, d), lambda c, j, ce, cn: (layer, ce[c], 0, 0))],
        out_specs=pl.BlockSpec(memory_space=pl.ANY),
        scratch_shapes=[pltpu.VMEM((rows, d), F32), pltpu.VMEM((rows, d), F32),
                        pltpu.SMEM((rows,), jnp.int32), pltpu.SMEM((rows,), jnp.int32),
                        pltpu.SemaphoreType.DMA((2,)), pltpu.SemaphoreType.DMA(()), pltpu.SemaphoreType.DMA(())])
    depth, ne = w1.shape[:2]
    return pl.pallas_call(
        functools.partial(_moe_kernel, rows=rows, sub=cfg.moe_sub, n_j=n_j),
        grid_spec=grid_spec,
        out_shape=jax.ShapeDtypeStruct((cfg.top_k * cfg.t, d), F32),
        compiler_params=_params(("arbitrary", "arbitrary")),
        name="moe_experts",
    )(e_c, n_c, tok, dst, h, w1, w1, b1.reshape(depth, ne, 1, 2 * de), b1.reshape(depth, ne, 1, 2 * de),
      w2, b2.reshape(depth, ne, 1, d))


def _moe_combine_kernel(x_ref, *rest, cfg, bm):
    ys, (g_ref, gate_ref, o_ref) = rest[:cfg.top_k], rest[cfg.top_k:]
    g = g_ref[...]
    acc = ys[0][...] * g[:, 0:1]
    for j in range(1, cfg.top_k):
        acc = acc + ys[j][...] * g[:, j:j + 1]
    o_ref[...] = x_ref[...] + gate_ref[pl.ds(_cond_row(pl.program_id(0) * bm, cfg), 1), :] * acc


def _moe_combine(x, buf, gates, mod, cfg, bm=128):
    t, d = x.shape
    nb = t // bm
    row = lambda i: (i, 0)
    y_specs = [pl.BlockSpec((bm, d), lambda i, j=j: (j * nb + i, 0)) for j in range(cfg.top_k)]
    return pl.pallas_call(
        functools.partial(_moe_combine_kernel, cfg=cfg, bm=bm),
        grid=(nb,),
        in_specs=[pl.BlockSpec((bm, d), row)] + y_specs +
                 [pl.BlockSpec((bm, LANES), row), pl.BlockSpec((MOD_ROWS, d), lambda i: (0, 5))],
        out_specs=pl.BlockSpec((bm, d), row),
        out_shape=jax.ShapeDtypeStruct((t, d), F32),
        compiler_params=_params(("arbitrary",)),
        name="moe_combine",
    )(x, *([buf] * cfg.top_k), gates, mod)


def _forward(cfg, x_prompt, x_sample, state_rwkv, c, c_ctx, ada_w, ada_b, norm1_w, norm2_w, final_norm_w,
             ab, hy, moe):
    d = cfg.d
    x = jnp.concatenate([x_prompt.reshape(cfg.t_ctx, d), x_sample.reshape(cfg.t_lat, d)], axis=0).astype(F32)
    cond = jnp.concatenate([c_ctx[None, :], c], axis=0)
    cond = jnp.pad(cond, ((0, MOD_ROWS - cond.shape[0]), (0, 0))).astype(F32)
    mods = _ada(cond, ada_w, ada_b, cfg)
    router_w, router_b, w1, b1, w2, b2 = moe
    finals = []
    for l in range(cfg.depth):
        mod = mods[l]
        h = _norm(x, norm1_w[l], cfg, mod=mod)
        if l % 2 == 0:
            i = l // 2
            x, s_fin = _mixer_ab(h, x, mod, state_rwkv[:, i], i % 2 == 1, {k: v[i] for k, v in ab.items()}, cfg)
            finals.append(s_fin)
        else:
            x = _mixer_hyena(h, x, mod, {k: v[l // 2] for k, v in hy.items()}, cfg)
        h2, top_i, gates = _norm_router(x, norm2_w[l], mod, router_w[l], router_b[l], cfg)
        buf = _moe_experts(h2, top_i[:, :cfg.top_k], l, w1, b1, w2, b2, cfg)
        x = _moe_combine(x, buf, gates, mod, cfg)
    y = _norm(x, final_norm_w, cfg, out_dtype=F32)
    y_prompt = y[:cfg.t_ctx].reshape(cfg.batch, cfg.seq, d)
    y_sample = y[cfg.t_ctx:].reshape(cfg.dec_batch, cfg.dec_seq, d)
    return y_prompt, y_sample, jnp.stack(finals, axis=1).astype(x_prompt.dtype)


_CFG = _make_cfg()


def kernel(x_prompt, x_sample, state_rwkv, c, c_ctx, ada_w, ada_b, norm1_w, norm2_w, final_norm_w, ab_w_in, ab_shift_mu, a_w0, a_w_up, a_a0, a_a_up, a_g_up, a_k_k, a_k_a, a_r_k, a_ln_w, a_ln_b, b_conv, ab_w_out, c_w_in, c_short_w, c_short_b, c_f_w0, c_f_b0, c_f_w1, c_f_b1, c_f_freq, c_f_out, c_bias, c_w_out, moe_router_w, moe_router_b, moe_w1, moe_b1, moe_w2, moe_b2):
    ab = dict(ab_w_in=ab_w_in, ab_shift_mu=ab_shift_mu, a_w0=a_w0, a_w_up=a_w_up, a_a0=a_a0, a_a_up=a_a_up,
              a_g_up=a_g_up, a_k_k=a_k_k, a_k_a=a_k_a, a_r_k=a_r_k.reshape(a_r_k.shape[0], -1), a_ln_w=a_ln_w,
              a_ln_b=a_ln_b, b_conv=b_conv, ab_w_out=ab_w_out)
    hy = dict(c_w_in=c_w_in, c_short_w=c_short_w, c_short_b=c_short_b, c_f_w0=c_f_w0, c_f_b0=c_f_b0,
              c_f_w1=c_f_w1, c_f_b1=c_f_b1, c_f_freq=c_f_freq, c_f_out=c_f_out, c_bias=c_bias, c_w_out=c_w_out)
    moe = (moe_router_w, moe_router_b, moe_w1, moe_b1, moe_w2, moe_b2)
    return _forward(_CFG, x_prompt, x_sample, state_rwkv, c, c_ctx, ada_w, ada_b, norm1_w, norm2_w, final_norm_w,
                    ab, hy, moe)
```

```python
import functools
import math
import types

import numpy as np
import jax
import jax.numpy as jnp
from jax import lax
from jax.experimental import pallas as pl
from jax.experimental.pallas import tpu as pltpu

F32 = jnp.float32
BF16 = jnp.bfloat16
HI = lax.Precision.HIGHEST

HEAD = 64
LANES = 128
ROW_TILE = 256
N_MOD = 6
EPS = 1e-6
GN_EPS = 64e-5
DECAY_SCALE = 0.606531
SWIGLU_ALPHA = 1.702
SWIGLU_LIMIT = 7.0
DECAY_TARGET = 1e-2
MIN_DECAY = math.log(DECAY_TARGET) / 0.3
MAX_DECAY = math.log(DECAY_TARGET) / 1.5
N_FILT_INNER = 2
HYENA_ORDER = 2
MOD_ROWS = 16
VMEM_LIMIT = 56 * 1024 * 1024
MOE_VMEM_LIMIT = 60 * 1024 * 1024
NEG = -1e30
DMA_UNROLL = 8


def _make_cfg(d_model=4096, batch=32, seq=256, depth=4, dec_batch=2, dec_seq=1024, grid_w=64,
              n_experts=32, top_k=4, d_expert=1536, w_lora=96, a_lora=96, g_lora=256,
              filt_bands=16, filt_hidden=64, moe_rows=1024, moe_sub=256, moe_bf=256, moe_bn=512):
    c = types.SimpleNamespace()
    c.d = d_model
    c.batch, c.seq, c.depth, c.dec_batch, c.dec_seq, c.grid_w = batch, seq, depth, dec_batch, dec_seq, grid_w
    c.n_even, c.n_odd = (depth + 1) // 2, depth // 2
    c.d_a = d_model // 2
    c.h_a = c.d_a // HEAD
    c.w_lora, c.a_lora, c.g_lora = w_lora, a_lora, g_lora
    c.n_a_cols = 3 * c.d_a + 2 * w_lora + 2 * a_lora + g_lora
    c.d_b = d_model - c.d_a
    c.d_c = d_model
    c.filt_emb = 2 * filt_bands + 1
    c.filt_bands, c.filt_hidden = filt_bands, filt_hidden
    c.n_experts, c.top_k, c.d_expert = n_experts, top_k, d_expert
    c.t_ctx, c.t_lat = batch * seq, dec_batch * dec_seq
    c.t = c.t_ctx + c.t_lat
    c.lo_pad = 4 * LANES + g_lora
    c.moe_rows, c.moe_sub, c.moe_bf, c.moe_bn = moe_rows, moe_sub, moe_bf, min(moe_bn, d_model)
    c.n_assign = c.t * top_k
    c.n_chunks = n_experts + c.n_assign // moe_rows
    assert LANES % c.h_a == 0 and batch % (LANES // c.h_a) == 0 and moe_rows % DMA_UNROLL == 0
    assert seq % ROW_TILE == 0 and dec_seq % ROW_TILE == 0 and ROW_TILE % grid_w == 0
    assert seq & (seq - 1) == 0 and grid_w & (grid_w - 1) == 0 and c.t_ctx % dec_seq == 0
    assert 1 + dec_batch <= MOD_ROWS and n_experts <= LANES
    return c


def _params(sem):
    return pltpu.CompilerParams(dimension_semantics=sem, vmem_limit_bytes=VMEM_LIMIT)


def _cond_row(row0, cfg):
    return jnp.where(row0 < cfg.t_ctx, 0, 1 + jnp.maximum(row0 - cfg.t_ctx, 0) // cfg.dec_seq)


def _neighbours(y, row0, cfg):
    n = y.shape[0]
    period = jnp.where(row0 < cfg.t_ctx, cfg.seq, cfg.grid_w)
    pos = lax.broadcasted_iota(jnp.int32, y.shape, 0) & (period - 1)
    prev = jnp.where(pos != 0, pltpu.roll(y, 1, 0), 0.0)
    nxt = jnp.where(pos != period - 1, pltpu.roll(y, n - 1, 0), 0.0)
    return prev, nxt


def _ada_kernel(c_ref, w_ref, b_ref, o_ref):
    c = c_ref[...]
    s = c * jax.nn.sigmoid(c)
    o_ref[...] = jnp.dot(s.astype(BF16), w_ref[...].astype(BF16), preferred_element_type=F32) + b_ref[...]


def _ada(cond, ada_w, ada_b, cfg, bn=512):
    depth, d, n = ada_w.shape
    bn = min(bn, n)
    return pl.pallas_call(
        _ada_kernel,
        grid=(depth, n // bn),
        in_specs=[pl.BlockSpec((MOD_ROWS, d), lambda l, j: (0, 0)),
                  pl.BlockSpec((None, d, bn), lambda l, j: (l, 0, j)),
                  pl.BlockSpec((None, 1, bn), lambda l, j: (l, 0, j))],
        out_specs=pl.BlockSpec((None, MOD_ROWS, bn), lambda l, j: (l, 0, j)),
        out_shape=jax.ShapeDtypeStruct((depth, MOD_ROWS, n), F32),
        compiler_params=_params(("arbitrary", "arbitrary")),
        name="ada_mod",
    )(cond, ada_w, ada_b.reshape(depth, 1, n))


def _norm_kernel(x_ref, g_ref, *rest, cfg, modulate):
    o_ref = rest[-1]
    x = x_ref[...]
    y = x * lax.rsqrt(jnp.mean(x * x, axis=-1, keepdims=True) + EPS) * g_ref[...]
    if modulate:
        sh_ref, sc_ref = rest[0], rest[1]
        r = _cond_row(pl.program_id(0) * ROW_TILE, cfg)
        y = y * (1.0 + sc_ref[pl.ds(r, 1), :]) + sh_ref[pl.ds(r, 1), :]
    o_ref[...] = y.astype(o_ref.dtype)


def _norm(x, g, cfg, mod=None, shift_blk=0, scale_blk=1, out_dtype=BF16):
    t, d = x.shape
    in_specs = [pl.BlockSpec((ROW_TILE, d), lambda i: (i, 0)), pl.BlockSpec((1, d), lambda i: (0, 0))]
    args = [x, g.reshape(1, d)]
    if mod is not None:
        in_specs += [pl.BlockSpec((MOD_ROWS, d), lambda i: (0, shift_blk)),
                     pl.BlockSpec((MOD_ROWS, d), lambda i: (0, scale_blk))]
        args += [mod, mod]
    return pl.pallas_call(
        functools.partial(_norm_kernel, cfg=cfg, modulate=mod is not None),
        grid=(t // ROW_TILE,),
        in_specs=in_specs,
        out_specs=pl.BlockSpec((ROW_TILE, d), lambda i: (i, 0)),
        out_shape=jax.ShapeDtypeStruct((t, d), out_dtype),
        compiler_params=_params(("arbitrary",)),
        name="rmsnorm_mod",
    )(*args)


def _norm_router_kernel(x_ref, g_ref, sh_ref, sc_ref, rw_ref, rb_ref, h_ref, ti_ref, tg_ref, *, cfg):
    x = x_ref[...]
    y = x * lax.rsqrt(jnp.mean(x * x, axis=-1, keepdims=True) + EPS) * g_ref[...]
    r = _cond_row(pl.program_id(0) * ROW_TILE, cfg)
    h = y * (1.0 + sc_ref[pl.ds(r, 1), :]) + sh_ref[pl.ds(r, 1), :]
    h_ref[...] = h
    logits = jnp.dot(h, rw_ref[...], preferred_element_type=F32, precision=HI) + rb_ref[...]
    lane = lax.broadcasted_iota(jnp.int32, logits.shape, 1).astype(F32)
    vals, idxs = [], []
    for _ in range(cfg.top_k):
        m = jnp.max(logits, axis=-1, keepdims=True)
        ij = jnp.min(jnp.where(logits == m, lane, float(LANES)), axis=-1, keepdims=True)
        vals.append(m)
        idxs.append(ij)
        logits = jnp.where(lane == ij, NEG, logits)
    es = [jnp.exp(v - vals[0]) for v in vals]
    tot = es[0]
    for e in es[1:]:
        tot = tot + e
    ti = jnp.zeros_like(lane)
    tg = jnp.zeros_like(lane)
    for j in range(cfg.top_k):
        ti = jnp.where(lane == float(j), idxs[j], ti)
        tg = jnp.where(lane == float(j), es[j] / tot, tg)
    ti_ref[...] = ti.astype(jnp.int32)
    tg_ref[...] = tg


def _norm_router(x, g, mod, router_w, router_b, cfg):
    t, d = x.shape
    e = cfg.n_experts
    rw = jnp.pad(router_w, ((0, 0), (0, LANES - e)))
    rb = jnp.pad(router_b, (0, LANES - e), constant_values=NEG).reshape(1, LANES)
    row = lambda i: (i, 0)
    return pl.pallas_call(
        functools.partial(_norm_router_kernel, cfg=cfg),
        grid=(t // ROW_TILE,),
        in_specs=[pl.BlockSpec((ROW_TILE, d), row), pl.BlockSpec((1, d), lambda i: (0, 0)),
                  pl.BlockSpec((MOD_ROWS, d), lambda i: (0, 3)), pl.BlockSpec((MOD_ROWS, d), lambda i: (0, 4)),
                  pl.BlockSpec((d, LANES), lambda i: (0, 0)), pl.BlockSpec((1, LANES), lambda i: (0, 0))],
        out_specs=[pl.BlockSpec((ROW_TILE, d), row), pl.BlockSpec((ROW_TILE, LANES), row),
                   pl.BlockSpec((ROW_TILE, LANES), row)],
        out_shape=[jax.ShapeDtypeStruct((t, d), F32), jax.ShapeDtypeStruct((t, LANES), jnp.int32),
                   jax.ShapeDtypeStruct((t, LANES), F32)],
        compiler_params=_params(("arbitrary",)),
        name="rmsnorm_router",
    )(x, g.reshape(1, d), mod, mod, rw, rb)


def _mm_kernel(x_ref, w_ref, *rest, cfg, bm, has_taps, has_res, hi):
    o_ref = rest[-1]
    if hi:
        y = jnp.dot(x_ref[...], w_ref[...], preferred_element_type=F32, precision=HI)
    else:
        y = jnp.dot(x_ref[...].astype(BF16), w_ref[...].astype(BF16), preferred_element_type=F32)
    row0 = pl.program_id(1) * bm
    k = 0
    if has_taps:
        taps = rest[k][...]
        k += 1
        prev, nxt = _neighbours(y, row0, cfg)
        y = prev * taps[0:1] + y * taps[1:2] + nxt * taps[2:3] + taps[3:4]
    if has_res:
        res_ref, gate_ref = rest[k], rest[k + 1]
        y = res_ref[...] + gate_ref[pl.ds(_cond_row(row0, cfg), 1), :] * y
    o_ref[...] = y.astype(o_ref.dtype)


def _mm(x, w, cfg, *, bm=512, bn=1024, out_dtype=F32, taps=None, res=None, gate=None, gate_blk=0, hi=False):
    m, kd = x.shape
    n = w.shape[1]
    bm, bn = min(bm, m), min(bn, n)
    while n % bn:
        bn //= 2
    assert m % bm == 0 and bn % LANES == 0
    in_specs = [pl.BlockSpec((bm, kd), lambda j, i: (i, 0)), pl.BlockSpec((kd, bn), lambda j, i: (0, j))]
    args = [x, w]
    if taps is not None:
        assert bm % cfg.seq == 0 and cfg.t_ctx % bm == 0
        in_specs.append(pl.BlockSpec((8, bn), lambda j, i: (0, j)))
        args.append(taps)
    if res is not None:
        assert cfg.t_ctx % bm == 0 and cfg.dec_seq % bm == 0
        nb = n // bn
        in_specs += [pl.BlockSpec((bm, bn), lambda j, i: (i, j)),
                     pl.BlockSpec((MOD_ROWS, bn), lambda j, i: (0, gate_blk * nb + j))]
        args += [res, gate]
    return pl.pallas_call(
        functools.partial(_mm_kernel, cfg=cfg, bm=bm, has_taps=taps is not None, has_res=res is not None, hi=hi),
        grid=(n // bn, m // bm),
        in_specs=in_specs,
        out_specs=pl.BlockSpec((bm, bn), lambda j, i: (i, j)),
        out_shape=jax.ShapeDtypeStruct((m, n), out_dtype),
        compiler_params=_params(("arbitrary", "arbitrary")),
        name="matmul",
    )(*args)


def _taps(t0, t1, t2, bias):
    z = jnp.zeros_like(t0)
    return jnp.stack([t0, t1, t2, bias, z, z, z, z]).astype(F32)


def _seg_mats(cfg):
    head = jnp.arange(cfg.d_a, dtype=jnp.int32) // HEAD
    seg = (head[:, None] == jnp.arange(LANES, dtype=jnp.int32)[None, :]).astype(F32)
    return seg, seg.T


def _head_sum(x, seg_ref, segt_ref):
    s = jnp.dot(x, seg_ref[...], preferred_element_type=F32, precision=HI)
    return jnp.dot(s, segt_ref[...], preferred_element_type=F32, precision=HI)


def _rwkv_prep_kernel(r_ref, k_ref, v_ref, lo_ref, kk_s, ka_s, rk_s, w0_ref, a0_ref, wup_ref, aup_ref, gup_ref,
                      seg_ref, segt_ref, a_out, w_out, k_out, b_out, bonus_out, g_out):
    r, k, v, lo = r_ref[...], k_ref[...], v_ref[...], lo_ref[...]
    kk = k * kk_s[...]
    norm = jnp.sqrt(_head_sum(kk * kk, seg_ref, segt_ref))
    kk = kk / jnp.maximum(norm, 1e-12)
    a_out[...] = -kk
    g_lo = lo[:, 4 * LANES:]
    g_out[...] = jnp.dot(jax.nn.sigmoid(g_lo).astype(BF16), gup_ref[...], preferred_element_type=F32)
    bonus = jnp.zeros_like(r)
    for d in range(2):
        w_lo = lo[:, d * LANES:(d + 1) * LANES]
        a_lo = lo[:, (2 + d) * LANES:(3 + d) * LANES]
        wz = w0_ref[d:d + 1, :] + jnp.dot(jnp.tanh(w_lo).astype(BF16), wup_ref[d], preferred_element_type=F32)
        w_out[d] = jnp.exp(-DECAY_SCALE * jax.nn.sigmoid(wz))
        a = jax.nn.sigmoid(a0_ref[d:d + 1, :] + jnp.dot(a_lo.astype(BF16), aup_ref[d], preferred_element_type=F32))
        k_d = k * (1.0 + (a - 1.0) * ka_s[...])
        k_out[d] = k_d
        b_out[d] = kk * a
        bonus = bonus + _head_sum(r * k_d * rk_s[...], seg_ref, segt_ref) * v
    bonus_out[...] = bonus


def _rwkv_prep(rkv, lo, p, cfg):
    t = rkv.shape[0]
    da = cfg.d_a
    seg, segt = _seg_mats(cfg)
    row = lambda i: (i, 0)
    full2 = lambda i: (0, 0)
    full3 = lambda i: (0, 0, 0)
    pad_k = lambda w: jnp.pad(w, ((0, 0), (0, LANES - w.shape[1]), (0, 0))).astype(BF16)
    out2 = jax.ShapeDtypeStruct((t, da), F32)
    out3 = jax.ShapeDtypeStruct((2, t, da), F32)
    tr = ROW_TILE // 2
    spec3 = pl.BlockSpec((2, tr, da), lambda i: (0, i, 0))
    return pl.pallas_call(
        _rwkv_prep_kernel,
        grid=(t // tr,),
        in_specs=[pl.BlockSpec((tr, da), lambda i: (i, 0)), pl.BlockSpec((tr, da), lambda i: (i, 1)),
                  pl.BlockSpec((tr, da), lambda i: (i, 2)), pl.BlockSpec((tr, cfg.lo_pad), row),
                  pl.BlockSpec((1, da), full2), pl.BlockSpec((1, da), full2), pl.BlockSpec((1, da), full2),
                  pl.BlockSpec((2, da), full2), pl.BlockSpec((2, da), full2),
                  pl.BlockSpec((2, LANES, da), full3), pl.BlockSpec((2, LANES, da), full3),
                  pl.BlockSpec((cfg.g_lora, da), full2),
                  pl.BlockSpec((da, LANES), full2), pl.BlockSpec((LANES, da), full2)],
        out_specs=[pl.BlockSpec((tr, da), row), spec3, spec3, spec3,
                   pl.BlockSpec((tr, da), row), pl.BlockSpec((tr, da), row)],
        out_shape=[out2, out3, out3, out3, out2, out2],
        compiler_params=_params(("arbitrary",)),
        name="rwkv_prep",
    )(rkv, rkv, rkv, lo, p['a_k_k'].reshape(1, da), p['a_k_a'].reshape(1, da), p['a_r_k'].reshape(1, da),
      p['a_w0'], p['a_a0'], pad_k(p['a_w_up']), pad_k(p['a_a_up']), p['a_g_up'].astype(BF16), seg, segt)


def _scan_kernel(r_ref, v_ref, a_ref, w_ref, k_ref, b_ref, s0_ref, o_ref, sf_ref, s_scr, *, tb, n_fwd):
    step_blk = pl.program_id(1)
    backward = pl.program_id(0) >= n_fwd
    hv = HEAD // 2

    @pl.when(step_blk == 0)
    def _():
        s_scr[...] = s0_ref[...]

    def step(i, carry):
        t = jnp.where(backward, tb - 1 - i, i)
        for half in range(2):
            vs = pl.ds(half * hv, hv)
            sa0 = jnp.zeros((hv, LANES), F32)
            sa1 = jnp.zeros((hv, LANES), F32)
            for k in range(0, HEAD, 2):
                sa0 = sa0 + s_scr[k, vs, :] * a_ref[t, pl.ds(k, 1), :]
                sa1 = sa1 + s_scr[k + 1, vs, :] * a_ref[t, pl.ds(k + 1, 1), :]
            sa = sa0 + sa1
            vv = v_ref[t, vs, :]
            o0 = jnp.zeros((hv, LANES), F32)
            o1 = jnp.zeros((hv, LANES), F32)
            for k in range(HEAD):
                sk = (s_scr[k, vs, :] * w_ref[t, pl.ds(k, 1), :] + sa * b_ref[t, pl.ds(k, 1), :]
                      + vv * k_ref[t, pl.ds(k, 1), :])
                s_scr[k, vs, :] = sk
                if k % 2 == 0:
                    o0 = o0 + sk * r_ref[t, pl.ds(k, 1), :]
                else:
                    o1 = o1 + sk * r_ref[t, pl.ds(k, 1), :]
            o_ref[t, vs, :] = o0 + o1
        return carry

    lax.fori_loop(0, tb, step, 0)

    @pl.when(step_blk == pl.num_programs(1) - 1)
    def _():
        sf_ref[...] = s_scr[...]


def _scan(shared, paired, s0, n_fwd, tb=16):
    g, l = paired[0].shape[:2]
    tb = min(tb, l)
    nt = l // tb
    tblk = lambda gi, ti: jnp.where(gi >= n_fwd, nt - 1 - ti, ti)
    pair_spec = pl.BlockSpec((None, tb, HEAD, LANES), lambda gi, ti: (gi, tblk(gi, ti), 0, 0))
    share_spec = pl.BlockSpec((None, tb, HEAD, LANES),
                              lambda gi, ti: (jnp.where(gi >= n_fwd, gi - n_fwd, gi), tblk(gi, ti), 0, 0))
    st_spec = pl.BlockSpec((None, HEAD, HEAD, LANES), lambda gi, ti: (gi, 0, 0, 0))
    return pl.pallas_call(
        functools.partial(_scan_kernel, tb=tb, n_fwd=n_fwd),
        grid=(g, nt),
        in_specs=[share_spec] * 3 + [pair_spec] * 3 + [st_spec],
        out_specs=[pair_spec, st_spec],
        out_shape=[jax.ShapeDtypeStruct((g, l, HEAD, LANES), F32), jax.ShapeDtypeStruct((g, HEAD, HEAD, LANES), F32)],
        scratch_shapes=[pltpu.VMEM((HEAD, HEAD, LANES), F32)],
        compiler_params=_params(("arbitrary", "arbitrary")),
        name="rwkv7_scan",
    )(*shared, *paired, s0)


def _ab_post_kernel(o_ref, bonus_ref, g_ref, bg_ref, cg_ref, hin_ref, lnw_ref, lnb_ref, bconv_ref, seg_ref, segt_ref,
                    y_ref, *, cfg):
    da = cfg.d_a
    o = o_ref[...]
    mean = _head_sum(o, seg_ref, segt_ref) * (1.0 / HEAD)
    c = o - mean
    var = _head_sum(c * c, seg_ref, segt_ref) * (1.0 / HEAD)
    on = c * lax.rsqrt(var + GN_EPS) * lnw_ref[...] + lnb_ref[...]
    y_ref[:, :da] = ((on + bonus_ref[...]) * g_ref[...]).astype(y_ref.dtype)
    z = cg_ref[...] * hin_ref[...]
    prev, nxt = _neighbours(z, pl.program_id(0) * ROW_TILE, cfg)
    w3 = bconv_ref[...]
    y_ref[:, da:] = (bg_ref[...] * (prev * w3[0:1] + z * w3[1:2] + nxt * w3[2:3])).astype(y_ref.dtype)


def _ab_post(o, bonus, g, zb, p, cfg):
    t = o.shape[0]
    da = cfg.d_a
    seg, segt = _seg_mats(cfg)
    row = lambda i: (i, 0)
    full2 = lambda i: (0, 0)
    half = pl.BlockSpec((ROW_TILE, da), row)
    bconv = jnp.pad(p['b_conv'], ((0, 5), (0, 0)))
    return pl.pallas_call(
        functools.partial(_ab_post_kernel, cfg=cfg),
        grid=(t // ROW_TILE,),
        in_specs=[half, half, half,
                  pl.BlockSpec((ROW_TILE, da), lambda i: (i, 0)), pl.BlockSpec((ROW_TILE, da), lambda i: (i, 1)),
                  pl.BlockSpec((ROW_TILE, da), lambda i: (i, 2)),
                  pl.BlockSpec((1, da), full2), pl.BlockSpec((1, da), full2), pl.BlockSpec((8, da), full2),
                  pl.BlockSpec((da, LANES), full2), pl.BlockSpec((LANES, da), full2)],
        out_specs=pl.BlockSpec((ROW_TILE, 2 * da), row),
        out_shape=jax.ShapeDtypeStruct((t, 2 * da), BF16),
        compiler_params=_params(("arbitrary",)),
        name="ab_post",
    )(o, bonus, g, zb, zb, zb, p['a_ln_w'].reshape(1, da), p['a_ln_b'].reshape(1, da), bconv, seg, segt)


def _col_major(z, cfg):
    b, l = z.shape[:2]
    return z.reshape((b, l // cfg.grid_w, cfg.grid_w) + z.shape[2:]).swapaxes(1, 2).reshape(z.shape)


def _row_major(z, cfg):
    b, l = z.shape[:2]
    return z.reshape((b, cfg.grid_w, l // cfg.grid_w) + z.shape[2:]).swapaxes(1, 2).reshape(z.shape)


def _to_chains(xf, xb, nb, l, col_major, cfg):
    xf = xf.reshape(nb, l, cfg.h_a, HEAD)
    xb = xb.reshape(nb, l, cfg.h_a, HEAD)
    if col_major:
        xf, xb = _col_major(xf, cfg), _col_major(xb, cfg)
    st = jnp.stack([xf, xb[:, ::-1]])
    n_ch = 2 * nb * cfg.h_a
    st = st.transpose(2, 4, 0, 1, 3).reshape(l, HEAD, n_ch)
    pad = -n_ch % LANES
    st = jnp.pad(st, ((0, 0), (0, 0), (0, pad)))
    return st.reshape(l, HEAD, (n_ch + pad) // LANES, LANES).transpose(2, 0, 1, 3)


def _from_chains(o, nb, l, col_major, cfg):
    n_ch = 2 * nb * cfg.h_a
    o = o.transpose(1, 2, 0, 3).reshape(l, HEAD, -1)[:, :, :n_ch]
    o = o.reshape(l, HEAD, 2, nb, cfg.h_a).transpose(2, 3, 0, 4, 1)
    of, ob = o[0], o[1][:, ::-1]
    if col_major:
        of, ob = _row_major(of, cfg), _row_major(ob, cfg)
    return (of + ob).reshape(nb * l, cfg.d_a)


def _state_to_chains(s, cfg):
    nb = s.shape[0]
    n_ch = 2 * nb * cfg.h_a
    s = s.transpose(4, 3, 1, 0, 2).reshape(HEAD, HEAD, n_ch)
    pad = -n_ch % LANES
    s = jnp.pad(s, ((0, 0), (0, 0), (0, pad)))
    return s.reshape(HEAD, HEAD, (n_ch + pad) // LANES, LANES).transpose(2, 0, 1, 3)


def _ctx_chains(x, cfg):
    gb = LANES // cfg.h_a
    lead = x.shape[:-2]
    n = len(lead)
    x = x.reshape(lead + (cfg.batch // gb, gb, cfg.seq, cfg.h_a, HEAD))
    x = x.transpose(tuple(range(n)) + (n, n + 2, n + 4, n + 1, n + 3))
    return x.reshape(lead + (cfg.batch // gb, cfg.seq, HEAD, LANES))


def _rwkv_scans(rkv, a_vec, w2, k2, b2, s0_lat, col_major, cfg):
    da, tc = cfg.d_a, cfg.t_ctx
    r, v = rkv[:, :da], rkv[:, 2 * da:3 * da]
    gb = LANES // cfg.h_a
    ng = cfg.batch // gb
    shared = tuple(_ctx_chains(x[:tc], cfg) for x in (r, v, a_vec))
    paired = tuple(_ctx_chains(x[:, :tc], cfg).reshape(2 * ng, cfg.seq, HEAD, LANES) for x in (w2, k2, b2))
    o, sf = _scan(shared, paired, jnp.zeros((2 * ng, HEAD, HEAD, LANES), F32), ng)
    o = o.reshape(2, ng, cfg.seq, HEAD, gb, cfg.h_a).transpose(0, 1, 4, 2, 5, 3)
    o_ctx = (o[0] + o[1]).reshape(tc, da)
    sf = sf.reshape(2, ng, HEAD, HEAD, gb, cfg.h_a).transpose(1, 4, 0, 5, 3, 2)
    final = sf.reshape(cfg.batch, 2, cfg.h_a, HEAD, HEAD)
    nb, l = cfg.dec_batch, cfg.dec_seq
    one = lambda x: _to_chains(x[tc:], x[tc:], nb, l, col_major, cfg)
    two = lambda x: _to_chains(x[0, tc:], x[1, tc:], nb, l, col_major, cfg)
    shared = (one(r), one(v), one(a_vec))
    o, _ = _scan(shared, (two(w2), two(k2), two(b2)), _state_to_chains(s0_lat.astype(F32), cfg), shared[0].shape[0])
    return jnp.concatenate([o_ctx, _from_chains(o, nb, l, col_major, cfg)], axis=0), final


def _mixer_ab(h, x, mod, s0_lat, col_major, p, cfg):
    da, nac = cfg.d_a, cfg.n_a_cols
    w_in = p['ab_w_in'].astype(BF16)
    mu = p['ab_shift_mu']
    one = jnp.ones_like(mu[0])
    zero = jnp.zeros_like(mu[0])
    taps = _taps(mu[0], one - mu[0] - mu[1], mu[1], zero)
    rkv = _mm(h, w_in[:, :3 * da], cfg, taps=taps[:, :3 * da])
    cuts = np.cumsum([3 * da, cfg.w_lora, cfg.w_lora, cfg.a_lora, cfg.a_lora]).tolist()
    padw = lambda a, lo_c, hi_c, fill: jnp.pad(a[:, lo_c:hi_c], ((0, 0), (0, LANES - (hi_c - lo_c))),
                                                constant_values=fill)
    w_lo = jnp.concatenate([padw(w_in, cuts[i], cuts[i + 1], 0) for i in range(4)] + [w_in[:, cuts[4]:nac]], axis=1)
    t_lo = jnp.concatenate([padw(taps, cuts[i], cuts[i + 1], 0) for i in range(4)] + [taps[:, cuts[4]:nac]], axis=1)
    lo = _mm(h, w_lo, cfg, bn=cfg.lo_pad, taps=t_lo)
    zb = _mm(h, w_in[:, nac:], cfg)
    a_vec, w2, k2, b2, bonus, g = _rwkv_prep(rkv, lo, p, cfg)
    o, final = _rwkv_scans(rkv, a_vec, w2, k2, b2, s0_lat, col_major, cfg)
    y = _ab_post(o, bonus, g, zb, p, cfg)
    x = _mm(y, p['ab_w_out'].astype(BF16), cfg, res=x, gate=mod, gate_blk=2)
    return x, final


def _filt_kernel(feat_ref, w0_ref, b0_ref, w1_ref, b1_ref, freq_ref, wout_ref, win_ref, o_ref, *, cols_per_dir):
    hid = jnp.sin(freq_ref[0:1, :] * (jnp.dot(feat_ref[...], w0_ref[...], preferred_element_type=F32, precision=HI)
                                      + b0_ref[...]))
    for j in range(N_FILT_INNER):
        hid = jnp.sin(freq_ref[j + 1:j + 2, :] * (jnp.dot(hid, w1_ref[j], preferred_element_type=F32, precision=HI)
                                                  + b1_ref[j:j + 1, :]))
    f = jnp.dot(hid, wout_ref[...], preferred_element_type=F32, precision=HI) * win_ref[...]
    backward = lax.rem(pl.program_id(0) // cols_per_dir, 2) == 1
    row = lax.broadcasted_iota(jnp.int32, f.shape, 0)
    o_ref[...] = jnp.where(jnp.logical_and(backward, row == 0), 0.0, f)


def _hyena_filters(l, p, cfg, bn=1024):
    dc = cfg.d_c
    bn = min(bn, dc)
    t = jnp.linspace(0.0, 1.0, l, dtype=F32)[:, None]
    ang = (2.0 * math.pi / l) * jnp.arange(l, dtype=F32)[:, None]
    bands = jnp.linspace(1e-4, cfg.filt_bands - 1, cfg.filt_bands, dtype=F32)
    feat = jnp.concatenate([t, jnp.cos(bands * ang), -jnp.sin(bands * ang)], axis=-1)
    deltas = jnp.abs(jnp.linspace(MIN_DECAY, MAX_DECAY, dc, dtype=F32))
    window = jnp.exp(-t * deltas)
    hp = LANES
    pad2 = lambda a, r, c: jnp.pad(a.astype(F32), ((0, r - a.shape[0]), (0, c - a.shape[1])))
    n_out = HYENA_ORDER * 2 * dc
    full2 = lambda j: (0, 0)
    return pl.pallas_call(
        functools.partial(_filt_kernel, cols_per_dir=dc // bn),
        grid=(n_out // bn,),
        in_specs=[pl.BlockSpec((l, hp), full2), pl.BlockSpec((hp, hp), full2), pl.BlockSpec((1, hp), full2),
                  pl.BlockSpec((N_FILT_INNER, hp, hp), lambda j: (0, 0, 0)), pl.BlockSpec((N_FILT_INNER, hp), full2),
                  pl.BlockSpec((N_FILT_INNER + 1, hp), full2), pl.BlockSpec((hp, bn), lambda j: (0, j)),
                  pl.BlockSpec((l, bn), lambda j: (0, lax.rem(j, dc // bn)))],
        out_specs=pl.BlockSpec((l, bn), lambda j: (0, j)),
        out_shape=jax.ShapeDtypeStruct((l, n_out), F32),
        compiler_params=_params(("arbitrary",)),
        name="hyena_filter",
    )(pad2(feat, l, hp), pad2(p['c_f_w0'], hp, hp), pad2(p['c_f_b0'][None], 1, hp),
      jnp.pad(p['c_f_w1'].astype(F32), ((0, 0), (0, hp - cfg.filt_hidden), (0, hp - cfg.filt_hidden))),
      pad2(p['c_f_b1'], N_FILT_INNER, hp), pad2(p['c_f_freq'], N_FILT_INNER + 1, hp),
      pad2(p['c_f_out'], hp, n_out), window)


def _dft_mats(l):
    f = jnp.arange(l, dtype=jnp.int32)[:, None]
    s = jnp.arange(l, dtype=jnp.int32)[None, :]
    ang = lax.rem(f * s, 2 * l).astype(F32) * (math.pi / l)
    nyq = jnp.where(lax.rem(s, 2) == 0, 1.0, -1.0).astype(F32)
    cos, sin = jnp.cos(ang), jnp.sin(ang)
    fwd = jnp.concatenate([cos, jnp.where(f == 0, nyq, -sin)], axis=0)
    cf = jnp.where(f == 0, 1.0, 2.0)
    inv = jnp.concatenate([cos * cf, jnp.where(f == 0, nyq, -2.0 * sin)], axis=0).T * (1.0 / (2 * l))
    return fwd, inv


def _hconv_kernel(fwd_ref, inv_ref, u_ref, gate_ref, kr_ref, ki_ref, bias_ref, o_ref, *, l):
    u = u_ref[...]
    spec = jnp.dot(fwd_ref[...], u.astype(BF16), preferred_element_type=F32)
    ur, ui = spec[:l], spec[l:]
    kr, ki = kr_ref[...], ki_ref[...]
    dc_row = lax.broadcasted_iota(jnp.int32, ur.shape, 0) == 0
    yr = ur * kr - jnp.where(dc_row, 0.0, ui * ki)
    yi = jnp.where(dc_row, ui * ki, ur * ki + ui * kr)
    y = jnp.dot(inv_ref[...], jnp.concatenate([yr, yi], axis=0).astype(BF16), preferred_element_type=F32)
    o_ref[...] = gate_ref[...] * (y + u * bias_ref[...])


def _hconv(u_arr, u_blk, gate_arr, gate_blk, kr, ki, bias, prev, l, row_blk0, n_seq, fwd, inv, cfg, bc=512):
    dc = cfg.d_c
    bc = min(bc, dc)
    nb = dc // bc
    const = lambda j, b: (0, 0)
    in_specs = [pl.BlockSpec((2 * l, l), const), pl.BlockSpec((l, 2 * l), const),
                pl.BlockSpec((l, bc), lambda j, b: (row_blk0 + b, u_blk * nb + j)),
                pl.BlockSpec((l, bc), lambda j, b: (row_blk0 + b, gate_blk * nb + j)),
                pl.BlockSpec((l, bc), lambda j, b: (0, j)), pl.BlockSpec((l, bc), lambda j, b: (0, j)),
                pl.BlockSpec((1, bc), lambda j, b: (0, j))]
    args = [fwd, inv, u_arr, gate_arr, kr, ki, bias.reshape(1, dc)]
    aliases = {}
    if prev is not None:
        in_specs.append(pl.BlockSpec(memory_space=pl.ANY))
        args.append(prev)
        aliases = {len(args) - 1: 0}

    def body(*refs):
        _hconv_kernel(*refs[:7], refs[-1], l=l)

    return pl.pallas_call(
        body,
        grid=(nb, n_seq),
        in_specs=in_specs,
        out_specs=pl.BlockSpec((l, bc), lambda j, b: (row_blk0 + b, j)),
        out_shape=jax.ShapeDtypeStruct((cfg.t, dc), F32),
        input_output_aliases=aliases,
        compiler_params=_params(("arbitrary", "arbitrary")),
        name="hyena_conv",
    )(*args)


def _mixer_hyena(h, x, mod, p, cfg):
    dc = cfg.d_c
    w3 = p['c_short_w']
    z = _mm(h, p['c_w_in'].astype(BF16), cfg, taps=_taps(w3[0], w3[1], w3[2], p['c_short_b']))
    groups = []
    for l, row_blk0, n_seq in ((cfg.seq, 0, cfg.batch), (cfg.dec_seq, cfg.t_ctx // cfg.dec_seq, cfg.dec_batch)):
        fwd, inv = _dft_mats(l)
        filt = _hyena_filters(l, p, cfg)
        spec = _mm(fwd, filt, cfg, bm=min(512, 2 * l), bn=512, hi=True).reshape(2 * l, HYENA_ORDER, 2, dc)
        sf, sb = spec[:, :, 0], spec[:, :, 1]
        kr = sf[:l] + sb[:l]
        first = (jnp.arange(l) == 0)[:, None, None]
        ki = jnp.where(first, sf[l:] + sb[l:], sf[l:] - sb[l:])
        groups.append((l, row_blk0, n_seq, fwd.astype(BF16), inv.astype(BF16), kr, ki))
    u_arr, u_blk = z, 0
    for o in range(HYENA_ORDER):
        out = jnp.zeros((cfg.t, dc), F32)
        for l, row_blk0, n_seq, fwd, inv, kr, ki in groups:
            out = _hconv(u_arr, u_blk, z, 1 + o, kr[:, o], ki[:, o], p['c_bias'][o], out, l, row_blk0, n_seq,
                         fwd, inv, cfg)
        u_arr, u_blk = out, 0
    return _mm(u_arr, p['c_w_out'].astype(BF16), cfg, res=x, gate=mod, gate_blk=2)


def _for_rows(count, fn):
    def body(i, carry):
        for u in range(DMA_UNROLL):
            fn(i * DMA_UNROLL + u)
        return carry
    lax.fori_loop(0, (count + DMA_UNROLL - 1) // DMA_UNROLL, body, 0)


def _moe_kernel(ce_ref, cn_ref, tok_hbm, x_hbm, w1g_ref, w1l_ref, b1g_ref, b1l_ref, w2_ref, b2_ref,
                o_ref, xs, act, tok_s, sem_idx, sem_g, *, sub, n_j1):
    c, s = pl.program_id(0), pl.program_id(1)
    n = cn_ref[c]
    nxt = cn_ref[c + 1]
    rows = xs.shape[0]

    def load_table(chunk):
        cp = pltpu.make_async_copy(tok_hbm.at[chunk], tok_s, sem_idx)
        cp.start()
        cp.wait()

    def row_in(r):
        return pltpu.make_async_copy(x_hbm.at[pl.ds(tok_s[r], 1)], xs.at[pl.ds(r, 1)], sem_g)

    @pl.when(jnp.logical_and(c == 0, s == 0))
    def _():
        xs[...] = jnp.zeros_like(xs)
        load_table(0)
        _for_rows(n, lambda r: row_in(r).start())

    @pl.when(jnp.logical_and(s == 0, n > 0))
    def _():
        _for_rows(n, lambda r: row_in(r).wait())

    @pl.when(jnp.logical_and(s == n_j1, nxt > 0))
    def _():
        load_table(c + 1)
        _for_rows(nxt, lambda r: row_in(r).start())

    for st in range(rows // sub):
        rs = pl.ds(st * sub, sub)

        @pl.when(jnp.logical_and(s < n_j1, n > st * sub))
        def _(rs=rs):
            xb = xs[rs, :].astype(BF16)
            ug = jnp.dot(xb, w1g_ref[...].astype(BF16), preferred_element_type=F32) + b1g_ref[...]
            ul = jnp.dot(xb, w1l_ref[...].astype(BF16), preferred_element_type=F32) + b1l_ref[...]
            glu = jnp.minimum(ug, SWIGLU_LIMIT)
            lin = jnp.clip(ul, -SWIGLU_LIMIT, SWIGLU_LIMIT)
            act[s, rs, :] = (glu * jax.nn.sigmoid(SWIGLU_ALPHA * glu) * (lin + 1.0)).astype(BF16)

        @pl.when(jnp.logical_and(s >= n_j1, n > st * sub))
        def _(rs=rs):
            y = jnp.broadcast_to(b2_ref[...], (sub, o_ref.shape[1]))
            for j in range(n_j1):
                y = y + jnp.dot(act[j, rs, :], w2_ref[j].astype(BF16), preferred_element_type=F32)
            o_ref[rs, :] = y

        @pl.when(jnp.logical_and(s >= n_j1, n <= st * sub))
        def _(rs=rs):
            o_ref[rs, :] = jnp.zeros((sub, o_ref.shape[1]), F32)


def _moe_plan(top_i, cfg):
    rows, nc, k = cfg.moe_rows, cfg.n_chunks, cfg.top_k
    flat_e = top_i.reshape(-1)
    order = jnp.argsort(flat_e).astype(jnp.int32)
    e_sorted = flat_e[order]
    counts = jnp.zeros((cfg.n_experts,), jnp.int32).at[flat_e].add(1)
    start = jnp.cumsum(counts) - counts
    n_ch = (counts + rows - 1) // rows
    ch_end = jnp.cumsum(n_ch)
    ch_base = ch_end - n_ch
    off = jnp.arange(cfg.n_assign, dtype=jnp.int32) - start[e_sorted]
    slot = ((ch_base[e_sorted] + off // rows) * rows + off % rows).astype(jnp.int32)
    tok = jnp.zeros((nc * rows,), jnp.int32).at[slot].set(order // k).reshape(nc, rows)
    pos = jnp.zeros((cfg.n_assign,), jnp.int32).at[order].set(slot)
    cid = jnp.arange(nc, dtype=jnp.int32)
    valid = cid < ch_end[-1]
    e_c = jnp.minimum(jnp.searchsorted(ch_end, cid, side='right'), cfg.n_experts - 1).astype(jnp.int32)
    e_c = jnp.where(valid, e_c, e_c[jnp.maximum(ch_end[-1] - 1, 0)])
    q = cid - ch_base[e_c]
    n_c = jnp.where(valid, jnp.clip(counts[e_c] - q * rows, 0, rows), 0).astype(jnp.int32)
    return e_c, jnp.concatenate([n_c, jnp.zeros((1,), jnp.int32)]), tok, pos


def _moe_experts(h, e_c, n_c, tok, layer, w1, b1, w2, b2, cfg):
    d, de, bf, rows, bn, nc = cfg.d, cfg.d_expert, cfg.moe_bf, cfg.moe_rows, cfg.moe_bn, cfg.n_chunks
    n_j1, n_j2 = de // bf, d // bn
    live = lambda c, cn: cn[c] > 0
    j1 = lambda c, s, cn: jnp.where(live(c, cn), jnp.minimum(s, n_j1 - 1), n_j1 - 1)
    j2 = lambda c, s, cn: jnp.where(live(c, cn), jnp.maximum(s - n_j1, 0), n_j2 - 1)
    grid_spec = pltpu.PrefetchScalarGridSpec(
        num_scalar_prefetch=2,
        grid=(nc, n_j1 + n_j2),
        in_specs=[pl.BlockSpec(memory_space=pl.ANY), pl.BlockSpec(memory_space=pl.ANY),
                  pl.BlockSpec((None, None, d, bf), lambda c, s, ce, cn: (layer, ce[c], 0, j1(c, s, cn))),
                  pl.BlockSpec((None, None, d, bf), lambda c, s, ce, cn: (layer, ce[c], 0, n_j1 + j1(c, s, cn))),
                  pl.BlockSpec((None, None, 1, bf), lambda c, s, ce, cn: (layer, ce[c], 0, j1(c, s, cn))),
                  pl.BlockSpec((None, None, 1, bf), lambda c, s, ce, cn: (layer, ce[c], 0, n_j1 + j1(c, s, cn))),
                  pl.BlockSpec((None, None, n_j1, bf, bn), lambda c, s, ce, cn: (layer, ce[c], 0, 0, j2(c, s, cn))),
                  pl.BlockSpec((None, None, 1, bn), lambda c, s, ce, cn: (layer, ce[c], 0, j2(c, s, cn)))],
        out_specs=pl.BlockSpec((rows, bn), lambda c, s, ce, cn: (c, jnp.maximum(s - n_j1, 0))),
        scratch_shapes=[pltpu.VMEM((rows, d), F32), pltpu.VMEM((n_j1, rows, bf), BF16),
                        pltpu.SMEM((rows,), jnp.int32), pltpu.SemaphoreType.DMA(()), pltpu.SemaphoreType.DMA(())])
    depth, ne = w1.shape[:2]
    return pl.pallas_call(
        functools.partial(_moe_kernel, sub=cfg.moe_sub, n_j1=n_j1),
        grid_spec=grid_spec,
        out_shape=jax.ShapeDtypeStruct((nc * rows, d), F32),
        compiler_params=pltpu.CompilerParams(dimension_semantics=("arbitrary", "arbitrary"),
                                             vmem_limit_bytes=MOE_VMEM_LIMIT),
        name="moe_experts",
    )(e_c, n_c, tok, h, w1, w1, b1.reshape(depth, ne, 1, 2 * de), b1.reshape(depth, ne, 1, 2 * de),
      w2.reshape(depth, ne, n_j1, bf, d), b2.reshape(depth, ne, 1, d))


def _moe_combine_kernel(pos_hbm, ys_hbm, x_ref, g_ref, gate_ref, o_ref, buf, pos_s, sem_idx, sem_g, *, cfg, bm):
    i, nt = pl.program_id(0), pl.num_programs(0)
    n_rows = cfg.top_k * bm
    slot = lax.rem(i, 2)

    def load_table(tile, sl):
        cp = pltpu.make_async_copy(pos_hbm.at[tile], pos_s.at[sl], sem_idx)
        cp.start()
        cp.wait()

    def row_in(sl, q):
        return pltpu.make_async_copy(ys_hbm.at[pl.ds(pos_s[sl, q], 1)], buf.at[sl, pl.ds(q, 1)], sem_g.at[sl])

    @pl.when(i == 0)
    def _():
        load_table(0, 0)
        _for_rows(n_rows, lambda q: row_in(0, q).start())

    @pl.when(i + 1 < nt)
    def _():
        load_table(i + 1, 1 - slot)
        _for_rows(n_rows, lambda q: row_in(1 - slot, q).start())

    _for_rows(n_rows, lambda q: row_in(slot, q).wait())
    g = g_ref[...]
    acc = buf[slot, pl.ds(0, bm), :] * g[:, 0:1]
    for j in range(1, cfg.top_k):
        acc = acc + buf[slot, pl.ds(j * bm, bm), :] * g[:, j:j + 1]
    o_ref[...] = x_ref[...] + gate_ref[pl.ds(_cond_row(i * bm, cfg), 1), :] * acc


def _moe_combine(x, ys, pos, gates, mod, cfg, bm=128):
    t, d = x.shape
    nb = t // bm
    k = cfg.top_k
    assert (k * bm) % DMA_UNROLL == 0
    table = pos.reshape(nb, bm, k).transpose(0, 2, 1).reshape(nb, k * bm)
    row = lambda i: (i, 0)
    return pl.pallas_call(
        functools.partial(_moe_combine_kernel, cfg=cfg, bm=bm),
        grid=(nb,),
        in_specs=[pl.BlockSpec(memory_space=pl.ANY), pl.BlockSpec(memory_space=pl.ANY),
                  pl.BlockSpec((bm, d), row), pl.BlockSpec((bm, LANES), row),
                  pl.BlockSpec((MOD_ROWS, d), lambda i: (0, 5))],
        out_specs=pl.BlockSpec((bm, d), row),
        out_shape=jax.ShapeDtypeStruct((t, d), F32),
        scratch_shapes=[pltpu.VMEM((2, k * bm, d), F32), pltpu.SMEM((2, k * bm), jnp.int32),
                        pltpu.SemaphoreType.DMA(()), pltpu.SemaphoreType.DMA((2,))],
        compiler_params=_params(("arbitrary",)),
        name="moe_combine",
    )(table, ys, x, gates, mod)


def _forward(cfg, x_prompt, x_sample, state_rwkv, c, c_ctx, ada_w, ada_b, norm1_w, norm2_w, final_norm_w,
             ab, hy, moe):
    d = cfg.d
    x = jnp.concatenate([x_prompt.reshape(cfg.t_ctx, d), x_sample.reshape(cfg.t_lat, d)], axis=0).astype(F32)
    cond = jnp.concatenate([c_ctx[None, :], c], axis=0)
    cond = jnp.pad(cond, ((0, MOD_ROWS - cond.shape[0]), (0, 0))).astype(F32)
    mods = _ada(cond, ada_w, ada_b, cfg)
    router_w, router_b, w1, b1, w2, b2 = moe
    finals = []
    for l in range(cfg.depth):
        mod = mods[l]
        h = _norm(x, norm1_w[l], cfg, mod=mod)
        if l % 2 == 0:
            i = l // 2
            x, s_fin = _mixer_ab(h, x, mod, state_rwkv[:, i], i % 2 == 1, {k: v[i] for k, v in ab.items()}, cfg)
            finals.append(s_fin)
        else:
            x = _mixer_hyena(h, x, mod, {k: v[l // 2] for k, v in hy.items()}, cfg)
        h2, top_i, gates = _norm_router(x, norm2_w[l], mod, router_w[l], router_b[l], cfg)
        e_c, n_c, tok, pos = _moe_plan(top_i[:, :cfg.top_k], cfg)
        ys = _moe_experts(h2, e_c, n_c, tok, l, w1, b1, w2, b2, cfg)
        x = _moe_combine(x, ys, pos, gates, mod, cfg)
    y = _norm(x, final_norm_w, cfg, out_dtype=F32)
    y_prompt = y[:cfg.t_ctx].reshape(cfg.batch, cfg.seq, d)
    y_sample = y[cfg.t_ctx:].reshape(cfg.dec_batch, cfg.dec_seq, d)
    return y_prompt, y_sample, jnp.stack(finals, axis=1).astype(x_prompt.dtype)


_CFG = _make_cfg()


def kernel(x_prompt, x_sample, state_rwkv, c, c_ctx, ada_w, ada_b, norm1_w, norm2_w, final_norm_w, ab_w_in, ab_shift_mu, a_w0, a_w_up, a_a0, a_a_up, a_g_up, a_k_k, a_k_a, a_r_k, a_ln_w, a_ln_b, b_conv, ab_w_out, c_w_in, c_short_w, c_short_b, c_f_w0, c_f_b0, c_f_w1, c_f_b1, c_f_freq, c_f_out, c_bias, c_w_out, moe_router_w, moe_router_b, moe_w1, moe_b1, moe_w2, moe_b2):
    ab = dict(ab_w_in=ab_w_in, ab_shift_mu=ab_shift_mu, a_w0=a_w0, a_w_up=a_w_up, a_a0=a_a0, a_a_up=a_a_up,
              a_g_up=a_g_up, a_k_k=a_k_k, a_k_a=a_k_a, a_r_k=a_r_k.reshape(a_r_k.shape[0], -1), a_ln_w=a_ln_w,
              a_ln_b=a_ln_b, b_conv=b_conv, ab_w_out=ab_w_out)
    hy = dict(c_w_in=c_w_in, c_short_w=c_short_w, c_short_b=c_short_b, c_f_w0=c_f_w0, c_f_b0=c_f_b0,
              c_f_w1=c_f_w1, c_f_b1=c_f_b1, c_f_freq=c_f_freq, c_f_out=c_f_out, c_bias=c_bias, c_w_out=c_w_out)
    moe = (moe_router_w, moe_router_b, moe_w1, moe_b1, moe_w2, moe_b2)
    return _forward(_CFG, x_prompt, x_sample, state_rwkv, c, c_ctx, ada_w, ada_b, norm1_w, norm2_w, final_norm_w,
                    ab, hy, moe)
```

```python
import functools
import math
import types

import numpy as np
import jax
import jax.numpy as jnp
from jax import lax
from jax.experimental import pallas as pl
from jax.experimental.pallas import tpu as pltpu

F32 = jnp.float32
BF16 = jnp.bfloat16
HI = lax.Precision.HIGHEST

HEAD = 64
LANES = 128
ROW_TILE = 256
N_MOD = 6
EPS = 1e-6
GN_EPS = 64e-5
DECAY_SCALE = 0.606531
SWIGLU_ALPHA = 1.702
SWIGLU_LIMIT = 7.0
DECAY_TARGET = 1e-2
MIN_DECAY = math.log(DECAY_TARGET) / 0.3
MAX_DECAY = math.log(DECAY_TARGET) / 1.5
N_FILT_INNER = 2
HYENA_ORDER = 2
MOD_ROWS = 16
VMEM_LIMIT = 56 * 1024 * 1024
MOE_VMEM_LIMIT = 60 * 1024 * 1024
NEG = -1e30
DMA_UNROLL = 8


def _make_cfg(d_model=4096, batch=32, seq=256, depth=4, dec_batch=2, dec_seq=1024, grid_w=64,
              n_experts=32, top_k=4, d_expert=1536, w_lora=96, a_lora=96, g_lora=256,
              filt_bands=16, filt_hidden=64, moe_rows=1024, moe_sub=256, moe_bf=256, moe_bn=512):
    c = types.SimpleNamespace()
    c.d = d_model
    c.batch, c.seq, c.depth, c.dec_batch, c.dec_seq, c.grid_w = batch, seq, depth, dec_batch, dec_seq, grid_w
    c.n_even, c.n_odd = (depth + 1) // 2, depth // 2
    c.d_a = d_model // 2
    c.h_a = c.d_a // HEAD
    c.w_lora, c.a_lora, c.g_lora = w_lora, a_lora, g_lora
    c.n_a_cols = 3 * c.d_a + 2 * w_lora + 2 * a_lora + g_lora
    c.d_b = d_model - c.d_a
    c.d_c = d_model
    c.filt_emb = 2 * filt_bands + 1
    c.filt_bands, c.filt_hidden = filt_bands, filt_hidden
    c.n_experts, c.top_k, c.d_expert = n_experts, top_k, d_expert
    c.t_ctx, c.t_lat = batch * seq, dec_batch * dec_seq
    c.t = c.t_ctx + c.t_lat
    c.lo_pad = 4 * LANES + g_lora
    c.moe_rows, c.moe_sub, c.moe_bf, c.moe_bn = moe_rows, moe_sub, moe_bf, min(moe_bn, d_model)
    c.n_assign = c.t * top_k
    c.n_chunks = n_experts + c.n_assign // moe_rows
    assert LANES % c.h_a == 0 and batch % (LANES // c.h_a) == 0 and moe_rows % DMA_UNROLL == 0
    assert seq % ROW_TILE == 0 and dec_seq % ROW_TILE == 0 and ROW_TILE % grid_w == 0
    assert seq & (seq - 1) == 0 and grid_w & (grid_w - 1) == 0 and c.t_ctx % dec_seq == 0
    assert 1 + dec_batch <= MOD_ROWS and n_experts <= LANES
    return c


def _params(sem):
    return pltpu.CompilerParams(dimension_semantics=sem, vmem_limit_bytes=VMEM_LIMIT)


def _cond_row(row0, cfg):
    return jnp.where(row0 < cfg.t_ctx, 0, 1 + jnp.maximum(row0 - cfg.t_ctx, 0) // cfg.dec_seq)


def _neighbours(y, row0, cfg):
    n = y.shape[0]
    period = jnp.where(row0 < cfg.t_ctx, cfg.seq, cfg.grid_w)
    pos = lax.broadcasted_iota(jnp.int32, y.shape, 0) & (period - 1)
    prev = jnp.where(pos != 0, pltpu.roll(y, 1, 0), 0.0)
    nxt = jnp.where(pos != period - 1, pltpu.roll(y, n - 1, 0), 0.0)
    return prev, nxt


def _ada_kernel(c_ref, w_ref, b_ref, o_ref):
    c = c_ref[...]
    s = c * jax.nn.sigmoid(c)
    o_ref[...] = jnp.dot(s.astype(BF16), w_ref[...].astype(BF16), preferred_element_type=F32) + b_ref[...]


def _ada(cond, ada_w, ada_b, cfg, bn=512):
    depth, d, n = ada_w.shape
    bn = min(bn, n)
    return pl.pallas_call(
        _ada_kernel,
        grid=(depth, n // bn),
        in_specs=[pl.BlockSpec((MOD_ROWS, d), lambda l, j: (0, 0)),
                  pl.BlockSpec((None, d, bn), lambda l, j: (l, 0, j)),
                  pl.BlockSpec((None, 1, bn), lambda l, j: (l, 0, j))],
        out_specs=pl.BlockSpec((None, MOD_ROWS, bn), lambda l, j: (l, 0, j)),
        out_shape=jax.ShapeDtypeStruct((depth, MOD_ROWS, n), F32),
        compiler_params=_params(("arbitrary", "arbitrary")),
        name="ada_mod",
    )(cond, ada_w, ada_b.reshape(depth, 1, n))


def _norm_kernel(x_ref, g_ref, *rest, cfg, modulate):
    o_ref = rest[-1]
    x = x_ref[...]
    y = x * lax.rsqrt(jnp.mean(x * x, axis=-1, keepdims=True) + EPS) * g_ref[...]
    if modulate:
        sh_ref, sc_ref = rest[0], rest[1]
        r = _cond_row(pl.program_id(0) * ROW_TILE, cfg)
        y = y * (1.0 + sc_ref[pl.ds(r, 1), :]) + sh_ref[pl.ds(r, 1), :]
    o_ref[...] = y.astype(o_ref.dtype)


def _norm(x, g, cfg, mod=None, shift_blk=0, scale_blk=1, out_dtype=BF16):
    t, d = x.shape
    in_specs = [pl.BlockSpec((ROW_TILE, d), lambda i: (i, 0)), pl.BlockSpec((1, d), lambda i: (0, 0))]
    args = [x, g.reshape(1, d)]
    if mod is not None:
        in_specs += [pl.BlockSpec((MOD_ROWS, d), lambda i: (0, shift_blk)),
                     pl.BlockSpec((MOD_ROWS, d), lambda i: (0, scale_blk))]
        args += [mod, mod]
    return pl.pallas_call(
        functools.partial(_norm_kernel, cfg=cfg, modulate=mod is not None),
        grid=(t // ROW_TILE,),
        in_specs=in_specs,
        out_specs=pl.BlockSpec((ROW_TILE, d), lambda i: (i, 0)),
        out_shape=jax.ShapeDtypeStruct((t, d), out_dtype),
        compiler_params=_params(("arbitrary",)),
        name="rmsnorm_mod",
    )(*args)


def _norm_router_kernel(x_ref, g_ref, sh_ref, sc_ref, rw_ref, rb_ref, h_ref, ti_ref, tg_ref, *, cfg):
    x = x_ref[...]
    y = x * lax.rsqrt(jnp.mean(x * x, axis=-1, keepdims=True) + EPS) * g_ref[...]
    r = _cond_row(pl.program_id(0) * ROW_TILE, cfg)
    h = y * (1.0 + sc_ref[pl.ds(r, 1), :]) + sh_ref[pl.ds(r, 1), :]
    h_ref[...] = h
    logits = jnp.dot(h, rw_ref[...], preferred_element_type=F32, precision=HI) + rb_ref[...]
    lane = lax.broadcasted_iota(jnp.int32, logits.shape, 1).astype(F32)
    vals, idxs = [], []
    for _ in range(cfg.top_k):
        m = jnp.max(logits, axis=-1, keepdims=True)
        ij = jnp.min(jnp.where(logits == m, lane, float(LANES)), axis=-1, keepdims=True)
        vals.append(m)
        idxs.append(ij)
        logits = jnp.where(lane == ij, NEG, logits)
    es = [jnp.exp(v - vals[0]) for v in vals]
    tot = es[0]
    for e in es[1:]:
        tot = tot + e
    ti = jnp.zeros_like(lane)
    tg = jnp.zeros_like(lane)
    for j in range(cfg.top_k):
        ti = jnp.where(lane == float(j), idxs[j], ti)
        tg = jnp.where(lane == float(j), es[j] / tot, tg)
    ti_ref[...] = ti.astype(jnp.int32)
    tg_ref[...] = tg


def _norm_router(x, g, mod, router_w, router_b, cfg):
    t, d = x.shape
    e = cfg.n_experts
    rw = jnp.pad(router_w, ((0, 0), (0, LANES - e)))
    rb = jnp.pad(router_b, (0, LANES - e), constant_values=NEG).reshape(1, LANES)
    row = lambda i: (i, 0)
    return pl.pallas_call(
        functools.partial(_norm_router_kernel, cfg=cfg),
        grid=(t // ROW_TILE,),
        in_specs=[pl.BlockSpec((ROW_TILE, d), row), pl.BlockSpec((1, d), lambda i: (0, 0)),
                  pl.BlockSpec((MOD_ROWS, d), lambda i: (0, 3)), pl.BlockSpec((MOD_ROWS, d), lambda i: (0, 4)),
                  pl.BlockSpec((d, LANES), lambda i: (0, 0)), pl.BlockSpec((1, LANES), lambda i: (0, 0))],
        out_specs=[pl.BlockSpec((ROW_TILE, d), row), pl.BlockSpec((ROW_TILE, LANES), row),
                   pl.BlockSpec((ROW_TILE, LANES), row)],
        out_shape=[jax.ShapeDtypeStruct((t, d), F32), jax.ShapeDtypeStruct((t, LANES), jnp.int32),
                   jax.ShapeDtypeStruct((t, LANES), F32)],
        compiler_params=_params(("arbitrary",)),
        name="rmsnorm_router",
    )(x, g.reshape(1, d), mod, mod, rw, rb)


def _mm_kernel(x_ref, w_ref, *rest, cfg, bm, has_taps, has_res, hi):
    o_ref = rest[-1]
    if hi:
        y = jnp.dot(x_ref[...], w_ref[...], preferred_element_type=F32, precision=HI)
    else:
        y = jnp.dot(x_ref[...].astype(BF16), w_ref[...].astype(BF16), preferred_element_type=F32)
    row0 = pl.program_id(1) * bm
    k = 0
    if has_taps:
        taps = rest[k][...]
        k += 1
        prev, nxt = _neighbours(y, row0, cfg)
        y = prev * taps[0:1] + y * taps[1:2] + nxt * taps[2:3] + taps[3:4]
    if has_res:
        res_ref, gate_ref = rest[k], rest[k + 1]
        y = res_ref[...] + gate_ref[pl.ds(_cond_row(row0, cfg), 1), :] * y
    o_ref[...] = y.astype(o_ref.dtype)


def _mm(x, w, cfg, *, bm=512, bn=1024, out_dtype=F32, taps=None, res=None, gate=None, gate_blk=0, hi=False):
    m, kd = x.shape
    n = w.shape[1]
    bm, bn = min(bm, m), min(bn, n)
    while n % bn:
        bn //= 2
    assert m % bm == 0 and bn % LANES == 0
    in_specs = [pl.BlockSpec((bm, kd), lambda j, i: (i, 0)), pl.BlockSpec((kd, bn), lambda j, i: (0, j))]
    args = [x, w]
    if taps is not None:
        assert bm % cfg.seq == 0 and cfg.t_ctx % bm == 0
        in_specs.append(pl.BlockSpec((8, bn), lambda j, i: (0, j)))
        args.append(taps)
    if res is not None:
        assert cfg.t_ctx % bm == 0 and cfg.dec_seq % bm == 0
        nb = n // bn
        in_specs += [pl.BlockSpec((bm, bn), lambda j, i: (i, j)),
                     pl.BlockSpec((MOD_ROWS, bn), lambda j, i: (0, gate_blk * nb + j))]
        args += [res, gate]
    return pl.pallas_call(
        functools.partial(_mm_kernel, cfg=cfg, bm=bm, has_taps=taps is not None, has_res=res is not None, hi=hi),
        grid=(n // bn, m // bm),
        in_specs=in_specs,
        out_specs=pl.BlockSpec((bm, bn), lambda j, i: (i, j)),
        out_shape=jax.ShapeDtypeStruct((m, n), out_dtype),
        compiler_params=_params(("arbitrary", "arbitrary")),
        name="matmul",
    )(*args)


def _taps(t0, t1, t2, bias):
    z = jnp.zeros_like(t0)
    return jnp.stack([t0, t1, t2, bias, z, z, z, z]).astype(F32)


def _seg_mats(cfg):
    head = jnp.arange(cfg.d_a, dtype=jnp.int32) // HEAD
    seg = (head[:, None] == jnp.arange(LANES, dtype=jnp.int32)[None, :]).astype(F32)
    return seg, seg.T


def _head_sum(x, seg_ref, segt_ref):
    s = jnp.dot(x, seg_ref[...], preferred_element_type=F32, precision=HI)
    return jnp.dot(s, segt_ref[...], preferred_element_type=F32, precision=HI)


def _rwkv_prep_kernel(r_ref, k_ref, v_ref, lo_ref, kk_s, ka_s, rk_s, w0_ref, a0_ref, wup_ref, aup_ref, gup_ref,
                      seg_ref, segt_ref, a_out, w_out, k_out, b_out, bonus_out, g_out):
    r, k, v, lo = r_ref[...], k_ref[...], v_ref[...], lo_ref[...]
    kk = k * kk_s[...]
    norm = jnp.sqrt(_head_sum(kk * kk, seg_ref, segt_ref))
    kk = kk / jnp.maximum(norm, 1e-12)
    a_out[...] = -kk
    g_lo = lo[:, 4 * LANES:]
    g_out[...] = jnp.dot(jax.nn.sigmoid(g_lo).astype(BF16), gup_ref[...], preferred_element_type=F32)
    bonus = jnp.zeros_like(r)
    for d in range(2):
        w_lo = lo[:, d * LANES:(d + 1) * LANES]
        a_lo = lo[:, (2 + d) * LANES:(3 + d) * LANES]
        wz = w0_ref[d:d + 1, :] + jnp.dot(jnp.tanh(w_lo).astype(BF16), wup_ref[d], preferred_element_type=F32)
        w_out[d] = jnp.exp(-DECAY_SCALE * jax.nn.sigmoid(wz))
        a = jax.nn.sigmoid(a0_ref[d:d + 1, :] + jnp.dot(a_lo.astype(BF16), aup_ref[d], preferred_element_type=F32))
        k_d = k * (1.0 + (a - 1.0) * ka_s[...])
        k_out[d] = k_d
        b_out[d] = kk * a
        bonus = bonus + _head_sum(r * k_d * rk_s[...], seg_ref, segt_ref) * v
    bonus_out[...] = bonus


def _rwkv_prep(rkv, lo, p, cfg):
    t = rkv.shape[0]
    da = cfg.d_a
    seg, segt = _seg_mats(cfg)
    row = lambda i: (i, 0)
    full2 = lambda i: (0, 0)
    full3 = lambda i: (0, 0, 0)
    pad_k = lambda w: jnp.pad(w, ((0, 0), (0, LANES - w.shape[1]), (0, 0))).astype(BF16)
    out2 = jax.ShapeDtypeStruct((t, da), F32)
    out3 = jax.ShapeDtypeStruct((2, t, da), F32)
    tr = ROW_TILE // 2
    spec3 = pl.BlockSpec((2, tr, da), lambda i: (0, i, 0))
    return pl.pallas_call(
        _rwkv_prep_kernel,
        grid=(t // tr,),
        in_specs=[pl.BlockSpec((tr, da), lambda i: (i, 0)), pl.BlockSpec((tr, da), lambda i: (i, 1)),
                  pl.BlockSpec((tr, da), lambda i: (i, 2)), pl.BlockSpec((tr, cfg.lo_pad), row),
                  pl.BlockSpec((1, da), full2), pl.BlockSpec((1, da), full2), pl.BlockSpec((1, da), full2),
                  pl.BlockSpec((2, da), full2), pl.BlockSpec((2, da), full2),
                  pl.BlockSpec((2, LANES, da), full3), pl.BlockSpec((2, LANES, da), full3),
                  pl.BlockSpec((cfg.g_lora, da), full2),
                  pl.BlockSpec((da, LANES), full2), pl.BlockSpec((LANES, da), full2)],
        out_specs=[pl.BlockSpec((tr, da), row), spec3, spec3, spec3,
                   pl.BlockSpec((tr, da), row), pl.BlockSpec((tr, da), row)],
        out_shape=[out2, out3, out3, out3, out2, out2],
        compiler_params=_params(("arbitrary",)),
        name="rwkv_prep",
    )(rkv, rkv, rkv, lo, p['a_k_k'].reshape(1, da), p['a_k_a'].reshape(1, da), p['a_r_k'].reshape(1, da),
      p['a_w0'], p['a_a0'], pad_k(p['a_w_up']), pad_k(p['a_a_up']), p['a_g_up'].astype(BF16), seg, segt)


def _scan_kernel(r_ref, v_ref, a_ref, w_ref, k_ref, b_ref, s0_ref, o_ref, sf_ref, s_scr, *, tb, n_fwd):
    step_blk = pl.program_id(1)
    backward = pl.program_id(0) >= n_fwd
    hv = HEAD // 2

    @pl.when(step_blk == 0)
    def _():
        s_scr[...] = s0_ref[...]

    def step(i, carry):
        t = jnp.where(backward, tb - 1 - i, i)
        for half in range(2):
            vs = pl.ds(half * hv, hv)
            sa0 = jnp.zeros((hv, LANES), F32)
            sa1 = jnp.zeros((hv, LANES), F32)
            for k in range(0, HEAD, 2):
                sa0 = sa0 + s_scr[k, vs, :] * a_ref[t, pl.ds(k, 1), :]
                sa1 = sa1 + s_scr[k + 1, vs, :] * a_ref[t, pl.ds(k + 1, 1), :]
            sa = sa0 + sa1
            vv = v_ref[t, vs, :]
            o0 = jnp.zeros((hv, LANES), F32)
            o1 = jnp.zeros((hv, LANES), F32)
            for k in range(HEAD):
                sk = (s_scr[k, vs, :] * w_ref[t, pl.ds(k, 1), :] + sa * b_ref[t, pl.ds(k, 1), :]
                      + vv * k_ref[t, pl.ds(k, 1), :])
                s_scr[k, vs, :] = sk
                if k % 2 == 0:
                    o0 = o0 + sk * r_ref[t, pl.ds(k, 1), :]
                else:
                    o1 = o1 + sk * r_ref[t, pl.ds(k, 1), :]
            o_ref[t, vs, :] = o0 + o1
        return carry

    lax.fori_loop(0, tb, step, 0)

    @pl.when(step_blk == pl.num_programs(1) - 1)
    def _():
        sf_ref[...] = s_scr[...]


def _scan(shared, paired, s0, n_fwd, tb=16):
    g, l = paired[0].shape[:2]
    tb = min(tb, l)
    nt = l // tb
    tblk = lambda gi, ti: jnp.where(gi >= n_fwd, nt - 1 - ti, ti)
    pair_spec = pl.BlockSpec((None, tb, HEAD, LANES), lambda gi, ti: (gi, tblk(gi, ti), 0, 0))
    share_spec = pl.BlockSpec((None, tb, HEAD, LANES),
                              lambda gi, ti: (jnp.where(gi >= n_fwd, gi - n_fwd, gi), tblk(gi, ti), 0, 0))
    st_spec = pl.BlockSpec((None, HEAD, HEAD, LANES), lambda gi, ti: (gi, 0, 0, 0))
    return pl.pallas_call(
        functools.partial(_scan_kernel, tb=tb, n_fwd=n_fwd),
        grid=(g, nt),
        in_specs=[share_spec] * 3 + [pair_spec] * 3 + [st_spec],
        out_specs=[pair_spec, st_spec],
        out_shape=[jax.ShapeDtypeStruct((g, l, HEAD, LANES), F32), jax.ShapeDtypeStruct((g, HEAD, HEAD, LANES), F32)],
        scratch_shapes=[pltpu.VMEM((HEAD, HEAD, LANES), F32)],
        compiler_params=_params(("arbitrary", "arbitrary")),
        name="rwkv7_scan",
    )(*shared, *paired, s0)


def _ab_post_kernel(o_ref, bonus_ref, g_ref, bg_ref, cg_ref, hin_ref, lnw_ref, lnb_ref, bconv_ref, seg_ref, segt_ref,
                    y_ref, *, cfg):
    da = cfg.d_a
    o = o_ref[...]
    mean = _head_sum(o, seg_ref, segt_ref) * (1.0 / HEAD)
    c = o - mean
    var = _head_sum(c * c, seg_ref, segt_ref) * (1.0 / HEAD)
    on = c * lax.rsqrt(var + GN_EPS) * lnw_ref[...] + lnb_ref[...]
    y_ref[:, :da] = ((on + bonus_ref[...]) * g_ref[...]).astype(y_ref.dtype)
    z = cg_ref[...] * hin_ref[...]
    prev, nxt = _neighbours(z, pl.program_id(0) * ROW_TILE, cfg)
    w3 = bconv_ref[...]
    y_ref[:, da:] = (bg_ref[...] * (prev * w3[0:1] + z * w3[1:2] + nxt * w3[2:3])).astype(y_ref.dtype)


def _ab_post(o, bonus, g, zb, p, cfg):
    t = o.shape[0]
    da = cfg.d_a
    seg, segt = _seg_mats(cfg)
    row = lambda i: (i, 0)
    full2 = lambda i: (0, 0)
    half = pl.BlockSpec((ROW_TILE, da), row)
    bconv = jnp.pad(p['b_conv'], ((0, 5), (0, 0)))
    return pl.pallas_call(
        functools.partial(_ab_post_kernel, cfg=cfg),
        grid=(t // ROW_TILE,),
        in_specs=[half, half, half,
                  pl.BlockSpec((ROW_TILE, da), lambda i: (i, 0)), pl.BlockSpec((ROW_TILE, da), lambda i: (i, 1)),
                  pl.BlockSpec((ROW_TILE, da), lambda i: (i, 2)),
                  pl.BlockSpec((1, da), full2), pl.BlockSpec((1, da), full2), pl.BlockSpec((8, da), full2),
                  pl.BlockSpec((da, LANES), full2), pl.BlockSpec((LANES, da), full2)],
        out_specs=pl.BlockSpec((ROW_TILE, 2 * da), row),
        out_shape=jax.ShapeDtypeStruct((t, 2 * da), BF16),
        compiler_params=_params(("arbitrary",)),
        name="ab_post",
    )(o, bonus, g, zb, zb, zb, p['a_ln_w'].reshape(1, da), p['a_ln_b'].reshape(1, da), bconv, seg, segt)


def _col_major(z, cfg):
    b, l = z.shape[:2]
    return z.reshape((b, l // cfg.grid_w, cfg.grid_w) + z.shape[2:]).swapaxes(1, 2).reshape(z.shape)


def _row_major(z, cfg):
    b, l = z.shape[:2]
    return z.reshape((b, cfg.grid_w, l // cfg.grid_w) + z.shape[2:]).swapaxes(1, 2).reshape(z.shape)


def _to_chains(xf, xb, nb, l, col_major, cfg):
    xf = xf.reshape(nb, l, cfg.h_a, HEAD)
    xb = xb.reshape(nb, l, cfg.h_a, HEAD)
    if col_major:
        xf, xb = _col_major(xf, cfg), _col_major(xb, cfg)
    st = jnp.stack([xf, xb[:, ::-1]])
    n_ch = 2 * nb * cfg.h_a
    st = st.transpose(2, 4, 0, 1, 3).reshape(l, HEAD, n_ch)
    pad = -n_ch % LANES
    st = jnp.pad(st, ((0, 0), (0, 0), (0, pad)))
    return st.reshape(l, HEAD, (n_ch + pad) // LANES, LANES).transpose(2, 0, 1, 3)


def _from_chains(o, nb, l, col_major, cfg):
    n_ch = 2 * nb * cfg.h_a
    o = o.transpose(1, 2, 0, 3).reshape(l, HEAD, -1)[:, :, :n_ch]
    o = o.reshape(l, HEAD, 2, nb, cfg.h_a).transpose(2, 3, 0, 4, 1)
    of, ob = o[0], o[1][:, ::-1]
    if col_major:
        of, ob = _row_major(of, cfg), _row_major(ob, cfg)
    return (of + ob).reshape(nb * l, cfg.d_a)


def _state_to_chains(s, cfg):
    nb = s.shape[0]
    n_ch = 2 * nb * cfg.h_a
    s = s.transpose(4, 3, 1, 0, 2).reshape(HEAD, HEAD, n_ch)
    pad = -n_ch % LANES
    s = jnp.pad(s, ((0, 0), (0, 0), (0, pad)))
    return s.reshape(HEAD, HEAD, (n_ch + pad) // LANES, LANES).transpose(2, 0, 1, 3)


def _ctx_chains(x, cfg):
    gb = LANES // cfg.h_a
    lead = x.shape[:-2]
    n = len(lead)
    x = x.reshape(lead + (cfg.batch // gb, gb, cfg.seq, cfg.h_a, HEAD))
    x = x.transpose(tuple(range(n)) + (n, n + 2, n + 4, n + 1, n + 3))
    return x.reshape(lead + (cfg.batch // gb, cfg.seq, HEAD, LANES))


def _rwkv_scans(rkv, a_vec, w2, k2, b2, s0_lat, col_major, cfg):
    da, tc = cfg.d_a, cfg.t_ctx
    r, v = rkv[:, :da], rkv[:, 2 * da:3 * da]
    gb = LANES // cfg.h_a
    ng = cfg.batch // gb
    shared = tuple(_ctx_chains(x[:tc], cfg) for x in (r, v, a_vec))
    paired = tuple(_ctx_chains(x[:, :tc], cfg).reshape(2 * ng, cfg.seq, HEAD, LANES) for x in (w2, k2, b2))
    o, sf = _scan(shared, paired, jnp.zeros((2 * ng, HEAD, HEAD, LANES), F32), ng)
    o = o.reshape(2, ng, cfg.seq, HEAD, gb, cfg.h_a).transpose(0, 1, 4, 2, 5, 3)
    o_ctx = (o[0] + o[1]).reshape(tc, da)
    sf = sf.reshape(2, ng, HEAD, HEAD, gb, cfg.h_a).transpose(1, 4, 0, 5, 3, 2)
    final = sf.reshape(cfg.batch, 2, cfg.h_a, HEAD, HEAD)
    nb, l = cfg.dec_batch, cfg.dec_seq
    one = lambda x: _to_chains(x[tc:], x[tc:], nb, l, col_major, cfg)
    two = lambda x: _to_chains(x[0, tc:], x[1, tc:], nb, l, col_major, cfg)
    shared = (one(r), one(v), one(a_vec))
    o, _ = _scan(shared, (two(w2), two(k2), two(b2)), _state_to_chains(s0_lat.astype(F32), cfg), shared[0].shape[0])
    return jnp.concatenate([o_ctx, _from_chains(o, nb, l, col_major, cfg)], axis=0), final


def _mixer_ab(h, x, mod, s0_lat, col_major, p, cfg):
    da, nac = cfg.d_a, cfg.n_a_cols
    w_in = p['ab_w_in'].astype(BF16)
    mu = p['ab_shift_mu']
    one = jnp.ones_like(mu[0])
    zero = jnp.zeros_like(mu[0])
    taps = _taps(mu[0], one - mu[0] - mu[1], mu[1], zero)
    rkv = _mm(h, w_in[:, :3 * da], cfg, taps=taps[:, :3 * da])
    cuts = np.cumsum([3 * da, cfg.w_lora, cfg.w_lora, cfg.a_lora, cfg.a_lora]).tolist()
    padw = lambda a, lo_c, hi_c, fill: jnp.pad(a[:, lo_c:hi_c], ((0, 0), (0, LANES - (hi_c - lo_c))),
                                                constant_values=fill)
    w_lo = jnp.concatenate([padw(w_in, cuts[i], cuts[i + 1], 0) for i in range(4)] + [w_in[:, cuts[4]:nac]], axis=1)
    t_lo = jnp.concatenate([padw(taps, cuts[i], cuts[i + 1], 0) for i in range(4)] + [taps[:, cuts[4]:nac]], axis=1)
    lo = _mm(h, w_lo, cfg, bn=cfg.lo_pad, taps=t_lo)
    zb = _mm(h, w_in[:, nac:], cfg)
    a_vec, w2, k2, b2, bonus, g = _rwkv_prep(rkv, lo, p, cfg)
    o, final = _rwkv_scans(rkv, a_vec, w2, k2, b2, s0_lat, col_major, cfg)
    y = _ab_post(o, bonus, g, zb, p, cfg)
    x = _mm(y, p['ab_w_out'].astype(BF16), cfg, res=x, gate=mod, gate_blk=2)
    return x, final


def _filt_kernel(feat_ref, w0_ref, b0_ref, w1_ref, b1_ref, freq_ref, wout_ref, win_ref, o_ref, *, cols_per_dir):
    hid = jnp.sin(freq_ref[0:1, :] * (jnp.dot(feat_ref[...], w0_ref[...], preferred_element_type=F32, precision=HI)
                                      + b0_ref[...]))
    for j in range(N_FILT_INNER):
        hid = jnp.sin(freq_ref[j + 1:j + 2, :] * (jnp.dot(hid, w1_ref[j], preferred_element_type=F32, precision=HI)
                                                  + b1_ref[j:j + 1, :]))
    f = jnp.dot(hid, wout_ref[...], preferred_element_type=F32, precision=HI) * win_ref[...]
    backward = lax.rem(pl.program_id(0) // cols_per_dir, 2) == 1
    row = lax.broadcasted_iota(jnp.int32, f.shape, 0)
    o_ref[...] = jnp.where(jnp.logical_and(backward, row == 0), 0.0, f)


def _hyena_filters(l, p, cfg, bn=1024):
    dc = cfg.d_c
    bn = min(bn, dc)
    t = jnp.linspace(0.0, 1.0, l, dtype=F32)[:, None]
    ang = (2.0 * math.pi / l) * jnp.arange(l, dtype=F32)[:, None]
    bands = jnp.linspace(1e-4, cfg.filt_bands - 1, cfg.filt_bands, dtype=F32)
    feat = jnp.concatenate([t, jnp.cos(bands * ang), -jnp.sin(bands * ang)], axis=-1)
    deltas = jnp.abs(jnp.linspace(MIN_DECAY, MAX_DECAY, dc, dtype=F32))
    window = jnp.exp(-t * deltas)
    hp = LANES
    pad2 = lambda a, r, c: jnp.pad(a.astype(F32), ((0, r - a.shape[0]), (0, c - a.shape[1])))
    n_out = HYENA_ORDER * 2 * dc
    full2 = lambda j: (0, 0)
    return pl.pallas_call(
        functools.partial(_filt_kernel, cols_per_dir=dc // bn),
        grid=(n_out // bn,),
        in_specs=[pl.BlockSpec((l, hp), full2), pl.BlockSpec((hp, hp), full2), pl.BlockSpec((1, hp), full2),
                  pl.BlockSpec((N_FILT_INNER, hp, hp), lambda j: (0, 0, 0)), pl.BlockSpec((N_FILT_INNER, hp), full2),
                  pl.BlockSpec((N_FILT_INNER + 1, hp), full2), pl.BlockSpec((hp, bn), lambda j: (0, j)),
                  pl.BlockSpec((l, bn), lambda j: (0, lax.rem(j, dc // bn)))],
        out_specs=pl.BlockSpec((l, bn), lambda j: (0, j)),
        out_shape=jax.ShapeDtypeStruct((l, n_out), F32),
        compiler_params=_params(("arbitrary",)),
        name="hyena_filter",
    )(pad2(feat, l, hp), pad2(p['c_f_w0'], hp, hp), pad2(p['c_f_b0'][None], 1, hp),
      jnp.pad(p['c_f_w1'].astype(F32), ((0, 0), (0, hp - cfg.filt_hidden), (0, hp - cfg.filt_hidden))),
      pad2(p['c_f_b1'], N_FILT_INNER, hp), pad2(p['c_f_freq'], N_FILT_INNER + 1, hp),
      pad2(p['c_f_out'], hp, n_out), window)


def _dft_mats(l):
    f = jnp.arange(l, dtype=jnp.int32)[:, None]
    s = jnp.arange(l, dtype=jnp.int32)[None, :]
    ang = lax.rem(f * s, 2 * l).astype(F32) * (math.pi / l)
    nyq = jnp.where(lax.rem(s, 2) == 0, 1.0, -1.0).astype(F32)
    cos, sin = jnp.cos(ang), jnp.sin(ang)
    fwd = jnp.concatenate([cos, jnp.where(f == 0, nyq, -sin)], axis=0)
    cf = jnp.where(f == 0, 1.0, 2.0)
    inv = jnp.concatenate([cos * cf, jnp.where(f == 0, nyq, -2.0 * sin)], axis=0).T * (1.0 / (2 * l))
    return fwd, inv


def _hconv_kernel(fwd_ref, inv_ref, u_ref, gate_ref, kr_ref, ki_ref, bias_ref, o_ref, *, l):
    u = u_ref[...]
    spec = jnp.dot(fwd_ref[...], u.astype(BF16), preferred_element_type=F32)
    ur, ui = spec[:l], spec[l:]
    kr, ki = kr_ref[...], ki_ref[...]
    dc_row = lax.broadcasted_iota(jnp.int32, ur.shape, 0) == 0
    yr = ur * kr - jnp.where(dc_row, 0.0, ui * ki)
    yi = jnp.where(dc_row, ui * ki, ur * ki + ui * kr)
    y = jnp.dot(inv_ref[...], jnp.concatenate([yr, yi], axis=0).astype(BF16), preferred_element_type=F32)
    o_ref[...] = gate_ref[...] * (y + u * bias_ref[...])


def _hconv(u_arr, u_blk, gate_arr, gate_blk, kr, ki, bias, prev, l, row_blk0, n_seq, fwd, inv, cfg, bc=512):
    dc = cfg.d_c
    bc = min(bc, dc)
    nb = dc // bc
    const = lambda j, b: (0, 0)
    in_specs = [pl.BlockSpec((2 * l, l), const), pl.BlockSpec((l, 2 * l), const),
                pl.BlockSpec((l, bc), lambda j, b: (row_blk0 + b, u_blk * nb + j)),
                pl.BlockSpec((l, bc), lambda j, b: (row_blk0 + b, gate_blk * nb + j)),
                pl.BlockSpec((l, bc), lambda j, b: (0, j)), pl.BlockSpec((l, bc), lambda j, b: (0, j)),
                pl.BlockSpec((1, bc), lambda j, b: (0, j))]
    args = [fwd, inv, u_arr, gate_arr, kr, ki, bias.reshape(1, dc)]
    aliases = {}
    if prev is not None:
        in_specs.append(pl.BlockSpec(memory_space=pl.ANY))
        args.append(prev)
        aliases = {len(args) - 1: 0}

    def body(*refs):
        _hconv_kernel(*refs[:7], refs[-1], l=l)

    return pl.pallas_call(
        body,
        grid=(nb, n_seq),
        in_specs=in_specs,
        out_specs=pl.BlockSpec((l, bc), lambda j, b: (row_blk0 + b, j)),
        out_shape=jax.ShapeDtypeStruct((cfg.t, dc), F32),
        input_output_aliases=aliases,
        compiler_params=_params(("arbitrary", "arbitrary")),
        name="hyena_conv",
    )(*args)


def _mixer_hyena(h, x, mod, p, cfg):
    dc = cfg.d_c
    w3 = p['c_short_w']
    z = _mm(h, p['c_w_in'].astype(BF16), cfg, taps=_taps(w3[0], w3[1], w3[2], p['c_short_b']))
    groups = []
    for l, row_blk0, n_seq in ((cfg.seq, 0, cfg.batch), (cfg.dec_seq, cfg.t_ctx // cfg.dec_seq, cfg.dec_batch)):
        fwd, inv = _dft_mats(l)
        filt = _hyena_filters(l, p, cfg)
        spec = _mm(fwd, filt, cfg, bm=min(512, 2 * l), bn=512, hi=True).reshape(2 * l, HYENA_ORDER, 2, dc)
        sf, sb = spec[:, :, 0], spec[:, :, 1]
        kr = sf[:l] + sb[:l]
        first = (jnp.arange(l) == 0)[:, None, None]
        ki = jnp.where(first, sf[l:] + sb[l:], sf[l:] - sb[l:])
        groups.append((l, row_blk0, n_seq, fwd.astype(BF16), inv.astype(BF16), kr, ki))
    u_arr, u_blk = z, 0
    for o in range(HYENA_ORDER):
        out = jnp.zeros((cfg.t, dc), F32)
        for l, row_blk0, n_seq, fwd, inv, kr, ki in groups:
            out = _hconv(u_arr, u_blk, z, 1 + o, kr[:, o], ki[:, o], p['c_bias'][o], out, l, row_blk0, n_seq,
                         fwd, inv, cfg)
        u_arr, u_blk = out, 0
    return _mm(u_arr, p['c_w_out'].astype(BF16), cfg, res=x, gate=mod, gate_blk=2)


def _for_rows(count, fn):
    def body(i, carry):
        for u in range(DMA_UNROLL):
            fn(i * DMA_UNROLL + u)
        return carry
    lax.fori_loop(0, (count + DMA_UNROLL - 1) // DMA_UNROLL, body, 0)


def _moe_kernel(ce_ref, cn_ref, tok_hbm, x_hbm, w1g_ref, w1l_ref, b1g_ref, b1l_ref, w2_ref, b2_ref,
                o_ref, xs, act, tok_s, sem_idx, sem_g, *, sub, n_j1):
    c, s = pl.program_id(0), pl.program_id(1)
    n = cn_ref[c]
    nxt = cn_ref[c + 1]
    rows = xs.shape[0]

    def load_table(chunk):
        cp = pltpu.make_async_copy(tok_hbm.at[chunk], tok_s, sem_idx)
        cp.start()
        cp.wait()

    def row_in(r):
        return pltpu.make_async_copy(x_hbm.at[pl.ds(tok_s[r], 1)], xs.at[pl.ds(r, 1)], sem_g)

    @pl.when(jnp.logical_and(c == 0, s == 0))
    def _():
        xs[...] = jnp.zeros_like(xs)
        load_table(0)
        _for_rows(n, lambda r: row_in(r).start())

    @pl.when(jnp.logical_and(s == 0, n > 0))
    def _():
        _for_rows(n, lambda r: row_in(r).wait())

    @pl.when(jnp.logical_and(s == n_j1, nxt > 0))
    def _():
        load_table(c + 1)
        _for_rows(nxt, lambda r: row_in(r).start())

    for st in range(rows // sub):
        rs = pl.ds(st * sub, sub)

        @pl.when(jnp.logical_and(s < n_j1, n > st * sub))
        def _(rs=rs):
            xb = xs[rs, :].astype(BF16)
            ug = jnp.dot(xb, w1g_ref[...].astype(BF16), preferred_element_type=F32) + b1g_ref[...]
            ul = jnp.dot(xb, w1l_ref[...].astype(BF16), preferred_element_type=F32) + b1l_ref[...]
            glu = jnp.minimum(ug, SWIGLU_LIMIT)
            lin = jnp.clip(ul, -SWIGLU_LIMIT, SWIGLU_LIMIT)
            act[s, rs, :] = (glu * jax.nn.sigmoid(SWIGLU_ALPHA * glu) * (lin + 1.0)).astype(BF16)

        @pl.when(jnp.logical_and(s >= n_j1, n > st * sub))
        def _(rs=rs):
            y = jnp.broadcast_to(b2_ref[...], (sub, o_ref.shape[1]))
            for j in range(n_j1):
                y = y + jnp.dot(act[j, rs, :], w2_ref[j].astype(BF16), preferred_element_type=F32)
            o_ref[rs, :] = y

        @pl.when(jnp.logical_and(s >= n_j1, n <= st * sub))
        def _(rs=rs):
            o_ref[rs, :] = jnp.zeros((sub, o_ref.shape[1]), F32)


def _moe_plan(top_i, cfg):
    rows, nc, k = cfg.moe_rows, cfg.n_chunks, cfg.top_k
    flat_e = top_i.reshape(-1)
    order = jnp.argsort(flat_e).astype(jnp.int32)
    e_sorted = flat_e[order]
    counts = jnp.zeros((cfg.n_experts,), jnp.int32).at[flat_e].add(1)
    start = jnp.cumsum(counts) - counts
    n_ch = (counts + rows - 1) // rows
    ch_end = jnp.cumsum(n_ch)
    ch_base = ch_end - n_ch
    off = jnp.arange(cfg.n_assign, dtype=jnp.int32) - start[e_sorted]
    slot = ((ch_base[e_sorted] + off // rows) * rows + off % rows).astype(jnp.int32)
    pos = slot[jnp.argsort(order)]
    cid = jnp.arange(nc, dtype=jnp.int32)
    valid = cid < ch_end[-1]
    e_c = jnp.minimum(jnp.searchsorted(ch_end, cid, side='right'), cfg.n_experts - 1).astype(jnp.int32)
    e_c = jnp.where(valid, e_c, e_c[jnp.maximum(ch_end[-1] - 1, 0)])
    q = cid - ch_base[e_c]
    n_c = jnp.where(valid, jnp.clip(counts[e_c] - q * rows, 0, rows), 0).astype(jnp.int32)
    r = jnp.arange(rows, dtype=jnp.int32)[None, :]
    src = order[jnp.clip((start[e_c] + q * rows)[:, None] + r, 0, cfg.n_assign - 1)]
    tok = jnp.where(r < n_c[:, None], src // k, 0).astype(jnp.int32)
    return e_c, jnp.concatenate([n_c, jnp.zeros((1,), jnp.int32)]), tok, pos


def _moe_experts(h, e_c, n_c, tok, layer, w1, b1, w2, b2, cfg):
    d, de, bf, rows, bn, nc = cfg.d, cfg.d_expert, cfg.moe_bf, cfg.moe_rows, cfg.moe_bn, cfg.n_chunks
    n_j1, n_j2 = de // bf, d // bn
    live = lambda c, cn: cn[c] > 0
    j1 = lambda c, s, cn: jnp.where(live(c, cn), jnp.minimum(s, n_j1 - 1), n_j1 - 1)
    j2 = lambda c, s, cn: jnp.where(live(c, cn), jnp.maximum(s - n_j1, 0), n_j2 - 1)
    grid_spec = pltpu.PrefetchScalarGridSpec(
        num_scalar_prefetch=2,
        grid=(nc, n_j1 + n_j2),
        in_specs=[pl.BlockSpec(memory_space=pl.ANY), pl.BlockSpec(memory_space=pl.ANY),
                  pl.BlockSpec((None, None, d, bf), lambda c, s, ce, cn: (layer, ce[c], 0, j1(c, s, cn))),
                  pl.BlockSpec((None, None, d, bf), lambda c, s, ce, cn: (layer, ce[c], 0, n_j1 + j1(c, s, cn))),
                  pl.BlockSpec((None, None, 1, bf), lambda c, s, ce, cn: (layer, ce[c], 0, j1(c, s, cn))),
                  pl.BlockSpec((None, None, 1, bf), lambda c, s, ce, cn: (layer, ce[c], 0, n_j1 + j1(c, s, cn))),
                  pl.BlockSpec((None, None, n_j1, bf, bn), lambda c, s, ce, cn: (layer, ce[c], 0, 0, j2(c, s, cn))),
                  pl.BlockSpec((None, None, 1, bn), lambda c, s, ce, cn: (layer, ce[c], 0, j2(c, s, cn)))],
        out_specs=pl.BlockSpec((rows, bn), lambda c, s, ce, cn: (c, jnp.maximum(s - n_j1, 0))),
        scratch_shapes=[pltpu.VMEM((rows, d), F32), pltpu.VMEM((n_j1, rows, bf), BF16),
                        pltpu.SMEM((rows,), jnp.int32), pltpu.SemaphoreType.DMA(()), pltpu.SemaphoreType.DMA(())])
    depth, ne = w1.shape[:2]
    return pl.pallas_call(
        functools.partial(_moe_kernel, sub=cfg.moe_sub, n_j1=n_j1),
        grid_spec=grid_spec,
        out_shape=jax.ShapeDtypeStruct((nc * rows, d), F32),
        compiler_params=pltpu.CompilerParams(dimension_semantics=("arbitrary", "arbitrary"),
                                             vmem_limit_bytes=MOE_VMEM_LIMIT),
        name="moe_experts",
    )(e_c, n_c, tok, h, w1, w1, b1.reshape(depth, ne, 1, 2 * de), b1.reshape(depth, ne, 1, 2 * de),
      w2.reshape(depth, ne, n_j1, bf, d), b2.reshape(depth, ne, 1, d))


def _moe_combine_kernel(pos_hbm, ys_hbm, x_ref, g_ref, gate_ref, o_ref, buf, pos_s, sem_idx, sem_g, *, cfg, bm):
    i, nt = pl.program_id(0), pl.num_programs(0)
    n_rows = cfg.top_k * bm
    slot = lax.rem(i, 2)

    def load_table(tile, sl):
        cp = pltpu.make_async_copy(pos_hbm.at[tile], pos_s.at[sl], sem_idx)
        cp.start()
        cp.wait()

    def row_in(sl, q):
        return pltpu.make_async_copy(ys_hbm.at[pl.ds(pos_s[sl, q], 1)], buf.at[sl, pl.ds(q, 1)], sem_g.at[sl])

    @pl.when(i == 0)
    def _():
        load_table(0, 0)
        _for_rows(n_rows, lambda q: row_in(0, q).start())

    @pl.when(i + 1 < nt)
    def _():
        load_table(i + 1, 1 - slot)
        _for_rows(n_rows, lambda q: row_in(1 - slot, q).start())

    _for_rows(n_rows, lambda q: row_in(slot, q).wait())
    g = g_ref[...]
    acc = buf[slot, pl.ds(0, bm), :] * g[:, 0:1]
    for j in range(1, cfg.top_k):
        acc = acc + buf[slot, pl.ds(j * bm, bm), :] * g[:, j:j + 1]
    o_ref[...] = x_ref[...] + gate_ref[pl.ds(_cond_row(i * bm, cfg), 1), :] * acc


def _moe_combine(x, ys, pos, gates, mod, cfg, bm=128):
    t, d = x.shape
    nb = t // bm
    k = cfg.top_k
    assert (k * bm) % DMA_UNROLL == 0
    table = pos.reshape(nb, bm, k).transpose(0, 2, 1).reshape(nb, k * bm)
    row = lambda i: (i, 0)
    return pl.pallas_call(
        functools.partial(_moe_combine_kernel, cfg=cfg, bm=bm),
        grid=(nb,),
        in_specs=[pl.BlockSpec(memory_space=pl.ANY), pl.BlockSpec(memory_space=pl.ANY),
                  pl.BlockSpec((bm, d), row), pl.BlockSpec((bm, LANES), row),
                  pl.BlockSpec((MOD_ROWS, d), lambda i: (0, 5))],
        out_specs=pl.BlockSpec((bm, d), row),
        out_shape=jax.ShapeDtypeStruct((t, d), F32),
        scratch_shapes=[pltpu.VMEM((2, k * bm, d), F32), pltpu.SMEM((2, k * bm), jnp.int32),
                        pltpu.SemaphoreType.DMA(()), pltpu.SemaphoreType.DMA((2,))],
        compiler_params=_params(("arbitrary",)),
        name="moe_combine",
    )(table, ys, x, gates, mod)


def _forward(cfg, x_prompt, x_sample, state_rwkv, c, c_ctx, ada_w, ada_b, norm1_w, norm2_w, final_norm_w,
             ab, hy, moe):
    d = cfg.d
    x = jnp.concatenate([x_prompt.reshape(cfg.t_ctx, d), x_sample.reshape(cfg.t_lat, d)], axis=0).astype(F32)
    cond = jnp.concatenate([c_ctx[None, :], c], axis=0)
    cond = jnp.pad(cond, ((0, MOD_ROWS - cond.shape[0]), (0, 0))).astype(F32)
    mods = _ada(cond, ada_w, ada_b, cfg)
    router_w, router_b, w1, b1, w2, b2 = moe
    finals = []
    for l in range(cfg.depth):
        mod = mods[l]
        h = _norm(x, norm1_w[l], cfg, mod=mod)
        if l % 2 == 0:
            i = l // 2
            x, s_fin = _mixer_ab(h, x, mod, state_rwkv[:, i], i % 2 == 1, {k: v[i] for k, v in ab.items()}, cfg)
            finals.append(s_fin)
        else:
            x = _mixer_hyena(h, x, mod, {k: v[l // 2] for k, v in hy.items()}, cfg)
        h2, top_i, gates = _norm_router(x, norm2_w[l], mod, router_w[l], router_b[l], cfg)
        e_c, n_c, tok, pos = _moe_plan(top_i[:, :cfg.top_k], cfg)
        ys = _moe_experts(h2, e_c, n_c, tok, l, w1, b1, w2, b2, cfg)
        x = _moe_combine(x, ys, pos, gates, mod, cfg)
    y = _norm(x, final_norm_w, cfg, out_dtype=F32)
    y_prompt = y[:cfg.t_ctx].reshape(cfg.batch, cfg.seq, d)
    y_sample = y[cfg.t_ctx:].reshape(cfg.dec_batch, cfg.dec_seq, d)
    return y_prompt, y_sample, jnp.stack(finals, axis=1).astype(x_prompt.dtype)


_CFG = _make_cfg()


def kernel(x_prompt, x_sample, state_rwkv, c, c_ctx, ada_w, ada_b, norm1_w, norm2_w, final_norm_w, ab_w_in, ab_shift_mu, a_w0, a_w_up, a_a0, a_a_up, a_g_up, a_k_k, a_k_a, a_r_k, a_ln_w, a_ln_b, b_conv, ab_w_out, c_w_in, c_short_w, c_short_b, c_f_w0, c_f_b0, c_f_w1, c_f_b1, c_f_freq, c_f_out, c_bias, c_w_out, moe_router_w, moe_router_b, moe_w1, moe_b1, moe_w2, moe_b2):
    ab = dict(ab_w_in=ab_w_in, ab_shift_mu=ab_shift_mu, a_w0=a_w0, a_w_up=a_w_up, a_a0=a_a0, a_a_up=a_a_up,
              a_g_up=a_g_up, a_k_k=a_k_k, a_k_a=a_k_a, a_r_k=a_r_k.reshape(a_r_k.shape[0], -1), a_ln_w=a_ln_w,
              a_ln_b=a_ln_b, b_conv=b_conv, ab_w_out=ab_w_out)
    hy = dict(c_w_in=c_w_in, c_short_w=c_short_w, c_short_b=c_short_b, c_f_w0=c_f_w0, c_f_b0=c_f_b0,
              c_f_w1=c_f_w1, c_f_b1=c_f_b1, c_f_freq=c_f_freq, c_f_out=c_f_out, c_bias=c_bias, c_w_out=c_w_out)
    moe = (moe_router_w, moe_router_b, moe_w1, moe_b1, moe_w2, moe_b2)
    return _forward(_CFG, x_prompt, x_sample, state_rwkv, c, c_ctx, ada_w, ada_b, norm1_w, norm2_w, final_norm_w,
                    ab, hy, moe)
```
